```python
import math
import jax, jax.numpy as jnp
from jax import lax
import numpy as np

D_MODEL = 1024
BATCH = 16
SEQ = 4096
DEPTH = 4

N_META = 16
N_BRANCH = 4
BRANCH_W = 256
S5_GROUP = 16
S5_GROUPS = BRANCH_W // S5_GROUP
S5_STATE = 64
GLA_HEADS = 4
GLA_DK = 32
GLA_DV = BRANCH_W // GLA_HEADS
GLA_GATE_RANK = 16
GLA_TAU = 16.0
GLA_CHUNK = 64
RW_HEADS = 4
RW_HEAD = BRANCH_W // RW_HEADS
RW_W_RANK = 32
RW_A_RANK = 32
RW_V_RANK = 16
RW_G_RANK = 64
RW_LN_EPS = 64e-5
LRU_HEADS = 4
LRU_BLOCK = BRANCH_W // LRU_HEADS
LRU_C = 8.0
LRU_CONV_W = 4
D_FF = 2816
FFN_CONV_W = 3
RMS_EPS = 1e-6

GLA_SIZES = (GLA_HEADS * GLA_DK, GLA_HEADS * GLA_DK, GLA_HEADS * GLA_DV, GLA_GATE_RANK, BRANCH_W)
RW_SIZES = (BRANCH_W, BRANCH_W, BRANCH_W, RW_W_RANK, RW_A_RANK, RW_G_RANK)
LRU_SIZES = (BRANCH_W, BRANCH_W)
GROUP_SIZES = (BRANCH_W, sum(GLA_SIZES), sum(RW_SIZES), sum(LRU_SIZES))
D_IN_PROJ = sum(GROUP_SIZES)

kernel_name = "hybrid_parallel_s5_gla_rwkv7_rglru_block"


def _split(t, sizes):
    return jnp.split(t, np.cumsum(sizes)[:-1].tolist(), axis=-1)


def _rmsnorm(x, g):
    xf = x.astype(jnp.float32)
    y = xf * lax.rsqrt(jnp.mean(xf * xf, axis=-1, keepdims=True) + RMS_EPS)
    return (y * g.astype(jnp.float32)).astype(x.dtype)


def _causal_dwconv(x, w, b):
    k, c = w.shape
    y = lax.conv_general_dilated(x, w[:, None, :].astype(x.dtype), window_strides=(1,), padding=[(k - 1, 0)], dimension_numbers=("NWC", "WIO", "NWC"), feature_group_count=c)
    return y + b.astype(x.dtype)


def _token_shift(p):
    return jnp.pad(p, ((0, 0), (1, 0), (0, 0)))[:, :-1]


def _complex_combine(e1, e2):
    a1r, a1i, b1r, b1i = e1
    a2r, a2i, b2r, b2i = e2
    return (a1r * a2r - a1i * a2i, a1r * a2i + a1i * a2r,
            a2r * b1r - a2i * b1i + b2r, a2r * b1i + a2i * b1r + b2i)


def _real_combine(e1, e2):
    a1, b1 = e1
    a2, b2 = e2
    return (a1 * a2, a2 * b1 + b2)


def _s5_branch(u, a_re, a_im, log_dt, b_re, b_im, c_re, c_im, d_skip, w_glu, b_glu):
    bsz, seq, _ = u.shape
    f32 = jnp.float32
    lr, li = a_re.astype(f32), a_im.astype(f32)
    dt = jnp.exp(log_dt.astype(f32))[:, None]
    mag = jnp.exp(lr * dt)
    abr, abi = mag * jnp.cos(li * dt), mag * jnp.sin(li * dt)
    den = lr * lr + li * li
    cr = ((abr - 1.0) * lr + abi * li) / den
    ci = (abi * lr - (abr - 1.0) * li) / den
    uf = u.astype(f32)
    ug = uf.reshape(bsz, seq, S5_GROUPS, S5_GROUP)
    bur = jnp.einsum("blgh,gnh->blgn", ug, b_re.astype(f32))
    bui = jnp.einsum("blgh,gnh->blgn", ug, b_im.astype(f32))
    xr = cr * bur - ci * bui
    xi = cr * bui + ci * bur
    shape_a = (1, seq, S5_GROUPS, S5_STATE)
    ar = jnp.broadcast_to(abr, shape_a)
    ai = jnp.broadcast_to(abi, shape_a)
    _, _, sr, si = lax.associative_scan(_complex_combine, (ar, ai, xr, xi), axis=1)
    y = jnp.einsum("blgn,ghn->blgh", sr, c_re.astype(f32)) - jnp.einsum("blgn,ghn->blgh", si, c_im.astype(f32))
    y = y.reshape(bsz, seq, BRANCH_W) + d_skip.astype(f32) * uf
    y = jax.nn.gelu(y)
    y = y * jax.nn.sigmoid(y @ w_glu.astype(f32) + b_glu.astype(f32))
    return y.astype(u.dtype)


def _gla_branch(p, w_gate_up, b_gate_up, norm_w):
    bsz, seq, _ = p.shape
    f32 = jnp.float32
    q, k, v, gd, og = _split(p.astype(f32), GLA_SIZES)
    g = jax.nn.log_sigmoid(gd @ w_gate_up.astype(f32) + b_gate_up.astype(f32)) / GLA_TAU
    pad = (-N_META) % GLA_CHUNK

    def chunks(t, d):
        t = jnp.pad(t, ((0, 0), (pad, 0), (0, 0)))
        return t.reshape(bsz, -1, GLA_CHUNK, GLA_HEADS, d)

    q = chunks(q, GLA_DK) * (GLA_DK ** -0.5)
    k = chunks(k, GLA_DK)
    v = chunks(v, GLA_DV)
    gcum = jnp.cumsum(chunks(g, GLA_DK), axis=2)
    g_last = gcum[:, :, -1]
    qg = q * jnp.exp(gcum)
    kg = k * jnp.exp(-gcum)
    kl = k * jnp.exp(g_last[:, :, None] - gcum)
    causal = jnp.tril(jnp.ones((GLA_CHUNK, GLA_CHUNK), dtype=bool))
    att = jnp.where(causal, jnp.einsum("bnthk,bnshk->bnhts", qg, kg), 0.0)
    o_intra = jnp.einsum("bnhts,bnshv->bnthv", att, v)
    d_state = jnp.einsum("bnshk,bnshv->bnhkv", kl, v)
    decay = jnp.exp(g_last)

    def step(s, inp):
        dec, ds = inp
        return s * dec[..., None] + ds, s

    s0 = jnp.zeros((bsz, GLA_HEADS, GLA_DK, GLA_DV), f32)
    _, s_start = lax.scan(step, s0, (jnp.moveaxis(decay, 1, 0), jnp.moveaxis(d_state, 1, 0)))
    s_start = jnp.moveaxis(s_start, 0, 1)
    o = o_intra + jnp.einsum("bnthk,bnhkv->bnthv", qg, s_start)
    o = o.reshape(bsz, -1, GLA_HEADS, GLA_DV)[:, pad:]
    o = o * lax.rsqrt(jnp.mean(o * o, axis=-1, keepdims=True) + RMS_EPS) * norm_w.astype(f32).reshape(GLA_HEADS, GLA_DV)
    o = o.reshape(bsz, seq, BRANCH_W) * jax.nn.silu(og)
    return o.astype(p.dtype)


def _rwkv7_branch(p, mu, w0, w2, a0, a2, g2, k_k, k_a, r_k, ln_w, ln_b, v_first, v_gate):
    bsz, seq, _ = p.shape
    f32 = jnp.float32
    pf = p.astype(f32)
    xs = pf + mu.astype(f32) * (_token_shift(pf) - pf)
    r, k, v, wd, ad, gd = _split(xs, RW_SIZES)
    w_log = -jax.nn.softplus(-(w0.astype(f32) + jnp.tanh(wd) @ w2.astype(f32))) - 0.5
    decay = jnp.exp(-jnp.exp(w_log))
    a = jax.nn.sigmoid(a0.astype(f32) + ad @ a2.astype(f32))
    g = jax.nn.sigmoid(gd) @ g2.astype(f32)
    if v_gate is None:
        v_first = v
    else:
        v = v + (v_first - v) * v_gate.astype(f32)
    heads = lambda t: t.reshape(bsz, seq, RW_HEADS, RW_HEAD)
    kk = heads(k * k_k.astype(f32))
    kk = kk * lax.rsqrt(jnp.maximum(jnp.sum(kk * kk, axis=-1, keepdims=True), 1e-24))
    k = k * (1.0 + (a - 1.0) * k_a.astype(f32))
    r_h, w_h, k_h, v_h, a_h = heads(r), heads(decay), heads(k), heads(v), heads(a)

    def step(s, inp):
        r_t, w_t, k_t, v_t, kk_t, a_t = inp
        sa = jnp.einsum("bhij,bhj->bhi", s, -kk_t)
        s = s * w_t[:, :, None, :] + sa[..., None] * (kk_t * a_t)[:, :, None, :] + v_t[..., None] * k_t[:, :, None, :]
        return s, jnp.einsum("bhij,bhj->bhi", s, r_t)

    s0 = jnp.zeros((bsz, RW_HEADS, RW_HEAD, RW_HEAD), f32)
    seq_first = [jnp.moveaxis(t, 1, 0) for t in (r_h, w_h, k_h, v_h, kk, a_h)]
    _, o = lax.scan(step, s0, tuple(seq_first))
    o = jnp.moveaxis(o, 0, 1)
    mean = jnp.mean(o, axis=-1, keepdims=True)
    var = jnp.mean(jnp.square(o - mean), axis=-1, keepdims=True)
    o = (o - mean) * lax.rsqrt(var + RW_LN_EPS) * ln_w.astype(f32).reshape(RW_HEADS, RW_HEAD) + ln_b.astype(f32).reshape(RW_HEADS, RW_HEAD)
    bonus = jnp.sum(r_h * k_h * r_k.astype(f32).reshape(RW_HEADS, RW_HEAD), axis=-1, keepdims=True) * v_h
    y = (o + bonus).reshape(bsz, seq, BRANCH_W) * g
    return y.astype(p.dtype), v_first


def _rglru_branch(p, conv_w, conv_b, w_a, b_a, w_i, b_i, lam):
    bsz, seq, _ = p.shape
    f32 = jnp.float32
    xr, gate = _split(p, LRU_SIZES)
    xc = _causal_dwconv(xr, conv_w, conv_b).astype(f32)
    xh = xc.reshape(bsz, seq, LRU_HEADS, LRU_BLOCK)
    rg = jax.nn.sigmoid(jnp.einsum("blhi,hij->blhj", xh, w_a.astype(f32)).reshape(bsz, seq, BRANCH_W) + b_a.astype(f32))
    ig = jax.nn.sigmoid(jnp.einsum("blhi,hij->blhj", xh, w_i.astype(f32)).reshape(bsz, seq, BRANCH_W) + b_i.astype(f32))
    log_a = -LRU_C * rg * jax.nn.softplus(-lam.astype(f32))
    a = jnp.exp(log_a)
    mult = jnp.sqrt(-jnp.expm1(2.0 * log_a))
    mult = mult.at[:, 0].set(1.0)
    _, h = lax.associative_scan(_real_combine, (a, mult * ig * xc), axis=1)
    return (h * jax.nn.gelu(gate.astype(f32))).astype(p.dtype)


def _conv_glu_ffn(x, w_up, conv_w, conv_b, w_down):
    gate, up = jnp.split(x @ w_up, 2, axis=-1)
    gate = _causal_dwconv(gate, conv_w, conv_b)
    return (jax.nn.silu(gate) * up) @ w_down


def setup_inputs(seed: int = 0) -> dict:
    key = jax.random.key(seed)
    ks = iter(jax.random.split(key, 64))
    f32 = jnp.float32

    def nrm(shape, scale):
        return jax.random.normal(next(ks), shape, f32) * scale

    def unif(shape, lo, hi):
        return jax.random.uniform(next(ks), shape, f32, lo, hi)

    L = DEPTH
    lru_u = unif((L, BRANCH_W), 0.9, 0.999)
    lru_a0 = lru_u ** (1.0 / LRU_C)
    return {
        "x": nrm((BATCH, SEQ, D_MODEL), 1.0),
        "meta": nrm((N_META, D_MODEL), 1.0),
        "w_in": nrm((L, D_MODEL, D_IN_PROJ), D_MODEL ** -0.5),
        "rw_mu": unif((L, sum(RW_SIZES)), 0.0, 1.0),
        "s5_a_re": -0.5 + nrm((L, S5_GROUPS, S5_STATE), 0.01),
        "s5_a_im": jnp.broadcast_to(jnp.pi * jnp.arange(S5_STATE, dtype=f32), (L, S5_GROUPS, S5_STATE)),
        "s5_log_dt": unif((L, S5_GROUPS), math.log(1e-3), math.log(1e-1)),
        "s5_b_re": nrm((L, S5_GROUPS, S5_STATE, S5_GROUP), (2 * S5_GROUP) ** -0.5),
        "s5_b_im": nrm((L, S5_GROUPS, S5_STATE, S5_GROUP), (2 * S5_GROUP) ** -0.5),
        "s5_c_re": nrm((L, S5_GROUPS, S5_GROUP, S5_STATE), S5_STATE ** -0.5),
        "s5_c_im": nrm((L, S5_GROUPS, S5_GROUP, S5_STATE), S5_STATE ** -0.5),
        "s5_d": nrm((L, BRANCH_W), 1.0),
        "s5_w_glu": nrm((L, BRANCH_W, BRANCH_W), BRANCH_W ** -0.5),
        "s5_b_glu": nrm((L, BRANCH_W), 0.01),
        "gla_w_gate": nrm((L, GLA_GATE_RANK, GLA_HEADS * GLA_DK), GLA_GATE_RANK ** -0.5),
        "gla_b_gate": nrm((L, GLA_HEADS * GLA_DK), 0.1),
        "gla_norm": 1.0 + nrm((L, BRANCH_W), 0.01),
        "rw_w0": jnp.linspace(-6.0, -1.0, BRANCH_W, dtype=f32) + nrm((L, BRANCH_W), 0.1),
        "rw_w2": nrm((L, RW_W_RANK, BRANCH_W), 0.5 * RW_W_RANK ** -0.5),
        "rw_a0": nrm((L, BRANCH_W), 0.1),
        "rw_a2": nrm((L, RW_A_RANK, BRANCH_W), RW_A_RANK ** -0.5),
        "rw_g2": nrm((L, RW_G_RANK, BRANCH_W), RW_G_RANK ** -0.5),
        "rw_k_k": 0.85 + nrm((L, BRANCH_W), 0.05),
        "rw_k_a": 1.0 + nrm((L, BRANCH_W), 0.05),
        "rw_r_k": nrm((L, BRANCH_W), 0.1),
        "rw_ln_w": 1.0 + nrm((L, BRANCH_W), 0.01),
        "rw_ln_b": nrm((L, BRANCH_W), 0.01),
        "rw_v0": 1.0 + nrm((L - 1, BRANCH_W), 0.1),
        "rw_v1": nrm((L - 1, D_MODEL, RW_V_RANK), D_MODEL ** -0.5),
        "rw_v2": nrm((L - 1, RW_V_RANK, BRANCH_W), RW_V_RANK ** -0.5),
        "lru_conv_w": nrm((L, LRU_CONV_W, BRANCH_W), LRU_CONV_W ** -0.5),
        "lru_conv_b": nrm((L, BRANCH_W), 0.01),
        "lru_w_a": nrm((L, LRU_HEADS, LRU_BLOCK, LRU_BLOCK), LRU_BLOCK ** -0.5),
        "lru_b_a": nrm((L, BRANCH_W), 0.01),
        "lru_w_i": nrm((L, LRU_HEADS, LRU_BLOCK, LRU_BLOCK), LRU_BLOCK ** -0.5),
        "lru_b_i": nrm((L, BRANCH_W), 0.01),
        "lru_lam": jnp.log(lru_a0) - jnp.log1p(-lru_a0),
        "w_branch": nrm((L, N_BRANCH, BRANCH_W, D_MODEL), BRANCH_W ** -0.5),
        "w_merge_gate": nrm((L, N_BRANCH, D_MODEL, D_MODEL), D_MODEL ** -0.5),
        "b_merge_gate": nrm((L, N_BRANCH, D_MODEL), 0.01),
        "w_out": nrm((L, D_MODEL, D_MODEL), D_MODEL ** -0.5),
        "norm_mix": 1.0 + nrm((L, D_MODEL), 0.01),
        "norm_ffn": 1.0 + nrm((L, D_MODEL), 0.01),
        "ffn_w_up": nrm((L, D_MODEL, 2 * D_FF), D_MODEL ** -0.5),
        "ffn_conv_w": nrm((L, FFN_CONV_W, D_FF), FFN_CONV_W ** -0.5),
        "ffn_conv_b": nrm((L, D_FF), 0.01),
        "ffn_w_down": nrm((L, D_FF, D_MODEL), D_FF ** -0.5),
        "norm_final": 1.0 + nrm((D_MODEL,), 0.01),
    }


def reference(x, meta, w_in, rw_mu, s5_a_re, s5_a_im, s5_log_dt, s5_b_re, s5_b_im, s5_c_re, s5_c_im, s5_d, s5_w_glu, s5_b_glu, gla_w_gate, gla_b_gate, gla_norm, rw_w0, rw_w2, rw_a0, rw_a2, rw_g2, rw_k_k, rw_k_a, rw_r_k, rw_ln_w, rw_ln_b, rw_v0, rw_v1, rw_v2, lru_conv_w, lru_conv_b, lru_w_a, lru_b_a, lru_w_i, lru_b_i, lru_lam, w_branch, w_merge_gate, b_merge_gate, w_out, norm_mix, norm_ffn, ffn_w_up, ffn_conv_w, ffn_conv_b, ffn_w_down, norm_final):
    bsz = x.shape[0]
    meta_b = jnp.broadcast_to(meta.astype(x.dtype)[None], (bsz, N_META, D_MODEL))
    h = jnp.concatenate([meta_b, x], axis=1)
    v_first = None
    for l in range(DEPTH):
        xn = _rmsnorm(h, norm_mix[l])
        p_s5, p_gla, p_rw, p_lru = _split(xn @ w_in[l], GROUP_SIZES)
        y_s5 = _s5_branch(p_s5, s5_a_re[l], s5_a_im[l], s5_log_dt[l], s5_b_re[l], s5_b_im[l], s5_c_re[l], s5_c_im[l], s5_d[l], s5_w_glu[l], s5_b_glu[l])
        y_gla = _gla_branch(p_gla, gla_w_gate[l], gla_b_gate[l], gla_norm[l])
        v_gate = None if l == 0 else jax.nn.sigmoid(rw_v0[l - 1] + (xn @ rw_v1[l - 1]) @ rw_v2[l - 1])
        y_rw, v_first = _rwkv7_branch(p_rw, rw_mu[l], rw_w0[l], rw_w2[l], rw_a0[l], rw_a2[l], rw_g2[l], rw_k_k[l], rw_k_a[l], rw_r_k[l], rw_ln_w[l], rw_ln_b[l], v_first, v_gate)
        y_lru = _rglru_branch(p_lru, lru_conv_w[l], lru_conv_b[l], lru_w_a[l], lru_b_a[l], lru_w_i[l], lru_b_i[l], lru_lam[l])
        branches = (y_s5, y_gla, y_rw, y_lru)
        merged = sum(jax.nn.sigmoid(xn @ w_merge_gate[l, b] + b_merge_gate[l, b]) * (branches[b] @ w_branch[l, b]) for b in range(N_BRANCH))
        h = h + merged @ w_out[l]
        h = h + _conv_glu_ffn(_rmsnorm(h, norm_ffn[l]), ffn_w_up[l], ffn_conv_w[l], ffn_conv_b[l], ffn_w_down[l])
    return _rmsnorm(h, norm_final)[:, N_META:]
```

```python
import functools
import math

import numpy as np
import jax
import jax.numpy as jnp
from jax import lax
from jax.experimental import pallas as pl
from jax.experimental.pallas import tpu as pltpu

F32 = jnp.float32
BF16 = jnp.bfloat16

D_MODEL = 1024
DEPTH = 4
N_META = 16
BRANCH_W = 256
S5_GROUP = 16
S5_GROUPS = 16
S5_STATE = 64
S5_BLOCK = 16
GLA_HEADS = 4
GLA_DK = 32
GLA_DV = 64
GLA_GATE_RANK = 16
GLA_TAU = 16.0
CHUNK = 64
RW_HEADS = 4
RW_HEAD = 64
RW_W_RANK = 32
RW_A_RANK = 32
RW_V_RANK = 16
RW_G_RANK = 64
RW_LN_EPS = 64e-5
LRU_C = 8.0
LRU_CONV_W = 4
D_FF = 2816
FFN_CONV_W = 3
RMS_EPS = 1e-6

PAD = (-N_META) % CHUNK
FRONT = PAD + N_META
LANES = 128
SUBLANES = 8
VMEM_LIMIT = 56 * 1024 * 1024

GLA_W = 896
RW_W = 896
RW_WV = 1024
LRU_W = 512


def _mm(a, b):
    return jnp.dot(a.astype(BF16), b.astype(BF16), preferred_element_type=F32)


def _mm_nt(a, b):
    return lax.dot_general(a.astype(BF16), b.astype(BF16), (((1,), (1,)), ((), ())),
                           preferred_element_type=F32)


def _mm_tn(a, b):
    return lax.dot_general(a.astype(BF16), b.astype(BF16), (((0,), (0,)), ((), ())),
                           preferred_element_type=F32)


def _split2(a):
    hi = a.astype(BF16)
    lo = (a - hi.astype(F32)).astype(BF16)
    return hi, lo


def _split3(a):
    hi = a.astype(BF16)
    r1 = a - hi.astype(F32)
    mid = r1.astype(BF16)
    lo = (r1 - mid.astype(F32)).astype(BF16)
    return hi, mid, lo


def _dg(a, b, dims):
    return lax.dot_general(a, b, (dims, ((), ())), preferred_element_type=F32)


def _mm3_dims(a, b, dims):
    ah, al = _split2(a)
    bh, bl = _split2(b)
    return _dg(ah, bh, dims) + (_dg(ah, bl, dims) + _dg(al, bh, dims))


_NN = ((1,), (0,))
_NT = ((1,), (1,))
_TN = ((0,), (0,))


def _mm3(a, b):
    return _mm3_dims(a, b, _NN)


def _mm3_nt(a, b):
    return _mm3_dims(a, b, _NT)


def _mm3_tn(a, b):
    return _mm3_dims(a, b, _TN)


def _mm_exact_rhs(a, b_bf16):
    hi, mid, lo = _split3(a)
    return _dg(hi, b_bf16, _NN) + (_dg(mid, b_bf16, _NN) + _dg(lo, b_bf16, _NN))


def _mm_exact_lhs(a_bf16, b):
    hi, mid, lo = _split3(b)
    return _dg(a_bf16, hi, _NN) + (_dg(a_bf16, mid, _NN) + _dg(a_bf16, lo, _NN))


def _softplus(x):
    return jnp.maximum(x, 0.0) + jnp.log1p(jnp.exp(-jnp.abs(x)))


def _sigmoid(x):
    return jax.nn.sigmoid(x)


def _silu(x):
    return x * jax.nn.sigmoid(x)


def _gelu(x):
    return jax.nn.gelu(x)


def _rmsnorm_rows(x, g):
    ms = jnp.mean(x * x, axis=-1, keepdims=True)
    return x * lax.rsqrt(ms + RMS_EPS) * g


def _iota(shape, dim):
    return lax.broadcasted_iota(jnp.int32, shape, dim)


def _div(x, n):
    return x >> int(math.log2(n))


def _mod(x, n):
    return x & (n - 1)


def _tile_rows4(x):
    return jnp.concatenate([x, x, x, x], axis=0)


def _fold_rows4(z):
    n = z.shape[0] // 4
    return (z[0:n] + z[n:2 * n]) + (z[2 * n:3 * n] + z[3 * n:4 * n])


def _shift_rows(x, s, carry8):
    xs = pltpu.roll(x, s, axis=0)
    top = jnp.where(_iota((SUBLANES, x.shape[1]), 0) < s,
                    pltpu.roll(carry8, s, axis=0), xs[0:SUBLANES])
    return jnp.concatenate([top, xs[SUBLANES:]], axis=0)


def _const_spec(shape):
    nd = len(shape)
    return pl.BlockSpec(shape, lambda *_: (0,) * nd)


def _row_spec(rows, width):
    return pl.BlockSpec((1, rows, width), lambda b, j: (b, j, 0))


def _params(sem):
    return pltpu.CompilerParams(dimension_semantics=sem, vmem_limit_bytes=VMEM_LIMIT)


def _pick_tile(lp, cap):
    best = CHUNK
    t = CHUNK
    while t <= min(lp, cap):
        if lp % t == 0:
            best = t
        t += CHUNK
    return best


def _proj_kernel(h_ref, g_ref, w_ref, *out_refs, widths):
    xn = _rmsnorm_rows(h_ref[0], g_ref[...]).astype(BF16)
    off = 0
    for o_ref, w in zip(out_refs, widths):
        o_ref[0] = jnp.dot(xn, w_ref[:, off:off + w], preferred_element_type=F32)
        off += w


def _proj(h, g, w, widths, tm):
    b, lp, d = h.shape
    nt = lp // tm
    return pl.pallas_call(
        functools.partial(_proj_kernel, widths=widths),
        grid=(b, nt),
        in_specs=[_row_spec(tm, d), _const_spec((1, d)), _const_spec(w.shape)],
        out_specs=[_row_spec(tm, wd) for wd in widths],
        out_shape=[jax.ShapeDtypeStruct((b, lp, wd), F32) for wd in widths],
        compiler_params=_params(("parallel", "parallel")),
        name="proj",
    )(h, g, w)


def _s5_kernel(u_ref, k_ref, e_ref, c_ref, a_ref, y_ref, e_scr, s_scr, *, nb, bsz):
    u = u_ref[0]
    e_scr[...] = _mm3(u, e_ref[0])
    pr = a_ref[0, 0:1, :]
    pi = a_ref[0, 1:2, :]

    def body(k, s):
        rows = pl.ds(pl.multiple_of(k * bsz, bsz), bsz)
        s_scr[rows, :] = s
        return pr * s + pi * pltpu.roll(s, S5_STATE, axis=1) + e_scr[rows, :]

    lax.fori_loop(0, nb, body, jnp.zeros((bsz, 2 * S5_STATE), F32))
    y_ref[0] = _mm3(u, k_ref[0]) + _mm3(s_scr[...], c_ref[0])


def _s5_mats(a_re, a_im, log_dt, b_re, b_im, c_re, c_im):
    lr, li = a_re.astype(F32), a_im.astype(F32)
    dt = jnp.exp(log_dt.astype(F32))[:, None]
    mag = jnp.exp(lr * dt)
    abr, abi = mag * jnp.cos(li * dt), mag * jnp.sin(li * dt)
    den = lr * lr + li * li
    cr = ((abr - 1.0) * lr + abi * li) / den
    ci = (abi * lr - (abr - 1.0) * li) / den
    d = jnp.arange(S5_BLOCK + 1, dtype=F32)[:, None, None]
    pmag = jnp.exp(lr * dt * d)
    p_re, p_im = pmag * jnp.cos(li * dt * d), pmag * jnp.sin(li * dt * d)
    bb_re = cr[..., None] * b_re - ci[..., None] * b_im
    bb_im = cr[..., None] * b_im + ci[..., None] * b_re
    cp_re = c_re[None] * p_re[:, :, None, :] - c_im[None] * p_im[:, :, None, :]
    cp_im = c_re[None] * p_im[:, :, None, :] + c_im[None] * p_re[:, :, None, :]
    hp = lax.Precision.HIGHEST
    kd = (jnp.einsum("dgon,gni->dgoi", cp_re, bb_re, precision=hp)
          - jnp.einsum("dgon,gni->dgoi", cp_im, bb_im, precision=hp))
    t = np.arange(S5_BLOCK)
    lag = t[None, :] - t[:, None]
    ksel = kd[np.clip(lag, 0, None)] * jnp.asarray(lag >= 0, F32)[:, :, None, None, None]
    kmat = ksel.transpose(2, 0, 4, 1, 3).reshape(S5_GROUPS, S5_BLOCK * S5_GROUP, S5_BLOCK * S5_GROUP)
    pe_re, pe_im = p_re[S5_BLOCK - 1 - t], p_im[S5_BLOCK - 1 - t]
    e_re = pe_re[..., None] * bb_re[None] - pe_im[..., None] * bb_im[None]
    e_im = pe_re[..., None] * bb_im[None] + pe_im[..., None] * bb_re[None]
    emat = jnp.concatenate([e_re, e_im], axis=2).transpose(1, 0, 3, 2)
    emat = emat.reshape(S5_GROUPS, S5_BLOCK * S5_GROUP, 2 * S5_STATE)
    cmat = jnp.concatenate([cp_re[1:], -cp_im[1:]], axis=3)
    cmat = cmat.transpose(1, 3, 0, 2).reshape(S5_GROUPS, 2 * S5_STATE, S5_BLOCK * S5_GROUP)
    a16 = jnp.stack([jnp.concatenate([p_re[S5_BLOCK], p_re[S5_BLOCK]], axis=-1),
                     jnp.concatenate([-p_im[S5_BLOCK], p_im[S5_BLOCK]], axis=-1)], axis=1)
    a16 = jnp.pad(a16, ((0, 0), (0, SUBLANES - 2), (0, 0)))
    return kmat, emat, cmat, a16


def _s5(p5, mats):
    kmat, emat, cmat, a16 = mats
    b, lp, _ = p5.shape
    nb = lp // S5_BLOCK
    gw = S5_BLOCK * S5_GROUP
    u = p5.reshape(b, nb, S5_BLOCK, S5_GROUPS, S5_GROUP).transpose(3, 1, 0, 2, 4)
    u = u.reshape(S5_GROUPS, nb * b, gw)
    gspec = lambda r, c: pl.BlockSpec((1, r, c), lambda g: (g, 0, 0))
    y = pl.pallas_call(
        functools.partial(_s5_kernel, nb=nb, bsz=b),
        grid=(S5_GROUPS,),
        in_specs=[gspec(nb * b, gw), gspec(gw, gw), gspec(gw, 2 * S5_STATE),
                  gspec(2 * S5_STATE, gw), gspec(SUBLANES, 2 * S5_STATE)],
        out_specs=gspec(nb * b, gw),
        out_shape=jax.ShapeDtypeStruct((S5_GROUPS, nb * b, gw), F32),
        scratch_shapes=[pltpu.VMEM((nb * b, 2 * S5_STATE), F32),
                        pltpu.VMEM((nb * b, 2 * S5_STATE), F32)],
        compiler_params=_params(("parallel",)),
        name="s5",
    )(u, kmat, emat, cmat, a16)
    y = y.reshape(S5_GROUPS, nb, b, S5_BLOCK, S5_GROUP).transpose(2, 1, 3, 0, 4)
    return y.reshape(b, lp, BRANCH_W)


def _gla_kernel(p_ref, wg_ref, bg_ref, nw_ref, y_ref, st_scr, *, cpt):
    @pl.when(pl.program_id(1) == 0)
    def _():
        st_scr[...] = jnp.zeros_like(st_scr)

    hq = GLA_HEADS * GLA_DK
    hv = GLA_HEADS * GLA_DV
    rows4 = GLA_HEADS * CHUNK
    tri = (_iota((CHUNK, CHUNK), 0) >= _iota((CHUNK, CHUNK), 1)).astype(BF16)
    bd_qk = _div(_iota((rows4, hq), 0), CHUNK) == _div(_iota((rows4, hq), 1), GLA_DK)
    causal = _mod(_iota((rows4, CHUNK), 0), CHUNK) >= _iota((rows4, CHUNK), 1)
    bd_v = _div(_iota((rows4, hv), 0), CHUNK) == _div(_iota((rows4, hv), 1), GLA_DV)
    bd_st = _div(_iota((hv, hq), 0), GLA_DV) == _div(_iota((hv, hq), 1), GLA_DK)
    head_ones = (_div(_iota((hv, hv), 0), GLA_DV) == _div(_iota((hv, hv), 1), GLA_DV)).astype(BF16)
    wg = wg_ref[...]
    bg = bg_ref[...]
    nw = nw_ref[...]

    def chunk(c, carry):
        rows = pl.ds(pl.multiple_of(c * CHUNK, CHUNK), CHUNK)
        p = p_ref[0, rows, :]
        q = p[:, 0:hq]
        k = p[:, hq:2 * hq]
        v = p[:, 2 * hq:2 * hq + hv]
        og = p[:, 2 * hq + hv:2 * hq + 2 * hv]
        gd = p[:, 2 * hq + 2 * hv:]
        x = _mm3(gd, wg) + bg
        g = (jnp.minimum(x, 0.0) - jnp.log1p(jnp.exp(-jnp.abs(x)))) / GLA_TAU
        gc = _mm_exact_lhs(tri, g)
        gl = gc[CHUNK - 1:CHUNK, :]
        qg = q * (GLA_DK ** -0.5) * jnp.exp(gc)
        kg = k * jnp.exp(-gc)
        kl = k * jnp.exp(gl - gc)
        qe = jnp.where(bd_qk, _tile_rows4(qg), 0.0)
        att = jnp.where(causal, _mm3_nt(qe, kg), 0.0)
        pe = jnp.where(bd_v, _mm3(att, v), 0.0)
        st = st_scr[...]
        o = _fold_rows4(pe) + _mm3_nt(qg, st)
        st_scr[...] = st * jnp.exp(gl) + jnp.where(bd_st, _mm3_tn(v, kl), 0.0)
        ms = _mm_exact_rhs(o * o, head_ones) * (1.0 / GLA_DV)
        o = o * lax.rsqrt(ms + RMS_EPS) * nw
        y_ref[0, rows, :] = o * _silu(og)
        return carry

    lax.fori_loop(0, cpt, chunk, 0)


def _gla(pg, w_gate_pad, b_gate, norm_w, tile):
    b, lp, _ = pg.shape
    cpt = tile // CHUNK
    return pl.pallas_call(
        functools.partial(_gla_kernel, cpt=cpt),
        grid=(b, lp // tile),
        in_specs=[_row_spec(tile, GLA_W), _const_spec(w_gate_pad.shape),
                  _const_spec(b_gate.shape), _const_spec(norm_w.shape)],
        out_specs=_row_spec(tile, BRANCH_W),
        out_shape=jax.ShapeDtypeStruct((b, lp, BRANCH_W), F32),
        scratch_shapes=[pltpu.VMEM((GLA_HEADS * GLA_DV, GLA_HEADS * GLA_DK), F32)],
        compiler_params=_params(("parallel", "arbitrary")),
        name="gla",
    )(pg, w_gate_pad, b_gate, norm_w)


def _rw_kernel(*refs, cpt, has_vgate):
    if has_vgate:
        (p_ref, vf_ref, mu_ref, w0_ref, w2_ref, a0_ref, a2_ref, g2_ref, kk_ref, ka_ref, rk_ref,
         lnw_ref, lnb_ref, v0_ref, v2_ref, y_ref, carry_scr, h_scr) = refs
    else:
        (p_ref, mu_ref, w0_ref, w2_ref, a0_ref, a2_ref, g2_ref, kk_ref, ka_ref, rk_ref,
         lnw_ref, lnb_ref, y_ref, vf_out_ref, carry_scr, h_scr) = refs

    @pl.when(pl.program_id(1) == 0)
    def _():
        carry_scr[...] = jnp.zeros_like(carry_scr)
        h_scr[...] = jnp.zeros_like(h_scr)

    w = BRANCH_W
    rows4 = RW_HEADS * CHUNK
    r_i = _iota((rows4, w), 0)
    c_i = _iota((rows4, w), 1)
    bd = _div(r_i, CHUNK) == _div(c_i, RW_HEAD)
    t_r = _mod(r_i, CHUNK)
    t_c = _mod(c_i, CHUNK)
    strict = bd & (t_c < t_r)
    incl = bd & (t_c <= t_r)
    eye = (r_i == c_i).astype(F32)
    tri = (_iota((CHUNK, CHUNK), 0) >= _iota((CHUNK, CHUNK), 1)).astype(BF16)
    head_ones = bd.astype(BF16)
    row0 = _iota((CHUNK, RW_W), 0) == 0
    mu = mu_ref[...]
    inv_n = 1.0 / RW_HEAD

    def expand(x):
        return jnp.where(bd, _tile_rows4(x), 0.0)

    def headsum(x):
        return _mm_exact_rhs(x, head_ones)

    def chunk(c, carry):
        rows = pl.ds(pl.multiple_of(c * CHUNK, CHUNK), CHUNK)
        p = p_ref[0, rows, :]
        pf = p[:, 0:RW_W]
        prev = jnp.where(row0, carry_scr[0:1, :], pltpu.roll(pf, 1, axis=0))
        carry_scr[0:1, :] = pf[CHUNK - 1:CHUNK, :]
        xs = pf + mu * (prev - pf)
        r = xs[:, 0:w]
        k = xs[:, w:2 * w]
        v = xs[:, 2 * w:3 * w]
        lr = xs[:, 3 * w:]
        w_log = -_softplus(-(w0_ref[...] + _mm3(jnp.tanh(lr), w2_ref[...]))) - 0.5
        lw = -jnp.exp(w_log)
        ag = _sigmoid(a0_ref[...] + _mm3(lr, a2_ref[...]))
        g = _mm3(_sigmoid(lr), g2_ref[...])
        if has_vgate:
            vg = _sigmoid(v0_ref[...] + _mm3(p[:, RW_W:], v2_ref[...]))
            v = v + (vf_ref[0, rows, :] - v) * vg
        else:
            vf_out_ref[0, rows, :] = v
        kk = k * kk_ref[...]
        kk = kk * lax.rsqrt(jnp.maximum(headsum(kk * kk), 1e-24))
        k2 = k * (1.0 + (ag - 1.0) * ka_ref[...])

        cum = _mm_exact_lhs(tri, lw)
        tot = cum[CHUNK - 1:CHUNK, :]
        e_neg = jnp.exp(-cum)
        e_tail = jnp.exp(tot - cum)
        bvec = kk * ag
        a_t = expand(-kk * jnp.exp(cum - lw))
        r_t = expand(r * jnp.exp(cum))
        lhs = jnp.concatenate([a_t, r_t], axis=0)
        rhs = jnp.concatenate([expand(bvec * e_neg), expand(k2 * e_neg)], axis=0)
        aa = _mm3_nt(lhs, rhs)
        a_ab = jnp.where(strict, aa[0:rows4, 0:w], 0.0)
        a_ak = jnp.where(strict, aa[0:rows4, w:], 0.0)
        a_rb = jnp.where(incl, aa[rows4:, 0:w], 0.0)
        a_rk = jnp.where(incl, aa[rows4:, w:], 0.0)

        tinv = eye + a_ab
        pw = a_ab
        for _ in range(5):
            pw = _mm3(pw, pw)
            tinv = tinv + _mm3(tinv, pw)

        v_e = expand(v)
        x = _mm3(tinv, jnp.concatenate([a_t, _mm3(a_ak, v_e)], axis=1))
        yq = _mm3(a_rb, x)
        q_n = _fold_rows4(r_t + yq[:, 0:w])
        ol_n = _fold_rows4(yq[:, w:] + _mm3(a_rk, v_e))
        x_n = _fold_rows4(x)
        mn = _mm3_tn(bvec * e_tail, x_n)
        m_mat = jnp.where(bd, mn[:, 0:w], 0.0) + eye * jnp.exp(tot)
        n_mat = jnp.where(bd, mn[:, w:] + _mm3_tn(k2 * e_tail, v), 0.0)
        h = h_scr[...]
        o = ol_n + _mm3(q_n, h)
        h_scr[...] = _mm3(m_mat, h) + n_mat

        mean = headsum(o) * inv_n
        dlt = o - mean
        var = headsum(dlt * dlt) * inv_n
        o = dlt * lax.rsqrt(var + RW_LN_EPS) * lnw_ref[...] + lnb_ref[...]
        bonus = headsum(r * k2 * rk_ref[...]) * v
        y_ref[0, rows, :] = (o + bonus) * g
        return carry

    lax.fori_loop(0, cpt, chunk, 0)


def _rwkv(pr, v_first, prm, tile):
    b, lp, pw = pr.shape
    has_vgate = v_first is not None
    cpt = tile // CHUNK
    vec = _const_spec((1, BRANCH_W))
    mat = _const_spec((LANES, BRANCH_W))
    ins = [pr]
    specs = [_row_spec(tile, pw)]
    if has_vgate:
        ins.append(v_first)
        specs.append(_row_spec(tile, BRANCH_W))
    ins += [prm["mu"], prm["w0"], prm["w2"], prm["a0"], prm["a2"], prm["g2"], prm["k_k"], prm["k_a"],
            prm["r_k"], prm["ln_w"], prm["ln_b"]]
    specs += [_const_spec((1, RW_W)), vec, mat, vec, mat, mat, vec, vec, vec, vec, vec]
    out_shape = [jax.ShapeDtypeStruct((b, lp, BRANCH_W), F32)]
    out_specs = [_row_spec(tile, BRANCH_W)]
    if has_vgate:
        ins += [prm["v0"], prm["v2"]]
        specs += [vec, mat]
    else:
        out_shape.append(jax.ShapeDtypeStruct((b, lp, BRANCH_W), F32))
        out_specs.append(_row_spec(tile, BRANCH_W))
    outs = pl.pallas_call(
        functools.partial(_rw_kernel, cpt=cpt, has_vgate=has_vgate),
        grid=(b, lp // tile),
        in_specs=specs,
        out_specs=out_specs,
        out_shape=out_shape,
        scratch_shapes=[pltpu.VMEM((SUBLANES, RW_W), F32),
                        pltpu.VMEM((RW_HEADS * RW_HEAD, BRANCH_W), F32)],
        compiler_params=_params(("parallel", "arbitrary")),
        name="rwkv7",
    )(*ins)
    if has_vgate:
        return outs[0], v_first
    return outs[0], outs[1]


def _lru_kernel(p_ref, cw_ref, cb_ref, wg_ref, bg_ref, lam_ref, y_ref, x_scr, h_scr, *, tl):
    j = pl.program_id(1)

    @pl.when(j == 0)
    def _():
        x_scr[...] = jnp.zeros_like(x_scr)
        h_scr[...] = jnp.zeros_like(h_scr)

    w = BRANCH_W
    p = p_ref[0]
    xr = p[:, 0:w]
    gate = p[:, w:]
    carry8 = x_scr[...]
    xc = cb_ref[...] + cw_ref[LRU_CONV_W - 1:LRU_CONV_W, :] * xr
    for s in range(1, LRU_CONV_W):
        xc = xc + cw_ref[LRU_CONV_W - 1 - s:LRU_CONV_W - s, :] * _shift_rows(xr, s, carry8)
    x_scr[...] = xr[tl - SUBLANES:tl, :]
    gates = _mm(xc, wg_ref[...]) + bg_ref[...]
    rg = _sigmoid(gates[:, 0:w])
    ig = _sigmoid(gates[:, w:])
    log_a = -LRU_C * rg * _softplus(-lam_ref[...])
    a = jnp.exp(log_a)
    mult = jnp.sqrt(1.0 - jnp.exp(2.0 * log_a))
    ridx = j * tl + _iota((tl, w), 0)
    mult = jnp.where(ridx == PAD, 1.0, mult)
    bterm = jnp.where(ridx >= PAD, mult * ig * xc, 0.0)
    t_i = _iota((tl, w), 0)
    d = 1
    while d < tl:
        keep = t_i >= d
        bterm = bterm + a * jnp.where(keep, pltpu.roll(bterm, d, axis=0), 0.0)
        a = a * jnp.where(keep, pltpu.roll(a, d, axis=0), 1.0)
        d *= 2
    h = bterm + a * h_scr[0:1, :]
    h_scr[0:1, :] = h[tl - 1:tl, :]
    y_ref[0] = h * _gelu(gate)


def _lru(pl_, cw, cb, wg, bg, lam, tile):
    b, lp, _ = pl_.shape
    vec = _const_spec((1, BRANCH_W))
    return pl.pallas_call(
        functools.partial(_lru_kernel, tl=tile),
        grid=(b, lp // tile),
        in_specs=[_row_spec(tile, LRU_W), _const_spec(cw.shape), vec, _const_spec(wg.shape),
                  _const_spec(bg.shape), vec],
        out_specs=_row_spec(tile, BRANCH_W),
        out_shape=jax.ShapeDtypeStruct((b, lp, BRANCH_W), F32),
        scratch_shapes=[pltpu.VMEM((SUBLANES, BRANCH_W), F32), pltpu.VMEM((SUBLANES, BRANCH_W), F32)],
        compiler_params=_params(("parallel", "arbitrary")),
        name="rglru",
    )(pl_, cw, cb, wg, bg, lam)


def _merge_kernel(h_ref, g_ref, u5_ref, y5_ref, yg_ref, yr_ref, yl_ref, d5_ref, wglu_ref, bglu_ref,
                  wgate_ref, bgate_ref, wbr_ref, wout_ref, o_ref, *, tm):
    x = h_ref[0]
    xn = _rmsnorm_rows(x, g_ref[...]).astype(BF16)
    ys = _gelu(y5_ref[0] + d5_ref[...] * u5_ref[0])
    ys = ys * _sigmoid(_mm(ys, wglu_ref[...]) + bglu_ref[...])
    branches = (ys, yg_ref[0], yr_ref[0], yl_ref[0])
    merged = None
    for i, yb in enumerate(branches):
        gate = _sigmoid(jnp.dot(xn, wgate_ref[i], preferred_element_type=F32) + bgate_ref[i])
        term = gate * _mm(yb, wbr_ref[i])
        merged = term if merged is None else merged + term
    upd = _mm(merged, wout_ref[...])
    ridx = pl.program_id(1) * tm + _iota((tm, 1), 0)
    o_ref[0] = x + jnp.where(ridx >= PAD, upd, 0.0)


def _merge(h, g, u5, y5, yg, yr, yl, d5, wglu, bglu, wgate, bgate, wbr, wout, tm):
    b, lp, d = h.shape
    ybs = _row_spec(tm, BRANCH_W)
    return pl.pallas_call(
        functools.partial(_merge_kernel, tm=tm),
        grid=(b, lp // tm),
        in_specs=[_row_spec(tm, d), _const_spec((1, d)), ybs, ybs, ybs, ybs, ybs,
                  _const_spec((1, BRANCH_W)), _const_spec(wglu.shape), _const_spec((1, BRANCH_W)),
                  _const_spec(wgate.shape), _const_spec(bgate.shape), _const_spec(wbr.shape),
                  _const_spec(wout.shape)],
        out_specs=_row_spec(tm, d),
        out_shape=jax.ShapeDtypeStruct((b, lp, d), F32),
        compiler_params=_params(("parallel", "parallel")),
        name="merge",
    )(h, g, u5, y5, yg, yr, yl, d5, wglu, bglu, wgate, bgate, wbr, wout)


FFN_COLS = 256


def _ffn_kernel(*refs, tm, final):
    if final:
        h_ref, g_ref, wup_ref, cw_ref, cb_ref, wdn_ref, gf_ref, o_ref, g_scr = refs
    else:
        h_ref, g_ref, wup_ref, cw_ref, cb_ref, wdn_ref, o_ref, g_scr = refs
    j = pl.program_id(1)

    @pl.when(j == 0)
    def _():
        g_scr[...] = jnp.zeros_like(g_scr)

    x = h_ref[0]
    xn = _rmsnorm_rows(x, g_ref[...]).astype(BF16)
    acc = jnp.zeros((tm, D_MODEL), F32)
    for c in range(D_FF // FFN_COLS):
        lo = c * FFN_COLS
        gate = jnp.dot(xn, wup_ref[:, lo:lo + FFN_COLS], preferred_element_type=F32)
        up = jnp.dot(xn, wup_ref[:, D_FF + lo:D_FF + lo + FFN_COLS], preferred_element_type=F32)
        carry8 = g_scr[:, lo:lo + FFN_COLS]
        g_scr[:, lo:lo + FFN_COLS] = gate[tm - SUBLANES:tm, :]
        gc = cb_ref[:, lo:lo + FFN_COLS] + cw_ref[FFN_CONV_W - 1:FFN_CONV_W, lo:lo + FFN_COLS] * gate
        for s in range(1, FFN_CONV_W):
            gc = gc + (cw_ref[FFN_CONV_W - 1 - s:FFN_CONV_W - s, lo:lo + FFN_COLS]
                       * _shift_rows(gate, s, carry8))
        acc = acc + _mm(_silu(gc) * up, wdn_ref[lo:lo + FFN_COLS, :])
    ridx = j * tm + _iota((tm, 1), 0)
    out = x + jnp.where(ridx >= PAD, acc, 0.0)
    if final:
        out = _rmsnorm_rows(out, gf_ref[...])
    o_ref[0] = out


def _ffn(h, g, wup, cw, cb, wdn, gf, tm):
    b, lp, d = h.shape
    final = gf is not None
    ins = [h, g, wup, cw, cb, wdn]
    specs = [_row_spec(tm, d), _const_spec((1, d)), _const_spec(wup.shape), _const_spec(cw.shape),
             _const_spec(cb.shape), _const_spec(wdn.shape)]
    if final:
        ins.append(gf)
        specs.append(_const_spec((1, d)))
    return pl.pallas_call(
        functools.partial(_ffn_kernel, tm=tm, final=final),
        grid=(b, lp // tm),
        in_specs=specs,
        out_specs=_row_spec(tm, d),
        out_shape=jax.ShapeDtypeStruct((b, lp, d), F32),
        scratch_shapes=[pltpu.VMEM((SUBLANES, D_FF), F32)],
        compiler_params=_params(("parallel", "arbitrary")),
        name="ffn",
    )(*ins)


def _pad_rows(m, rows, at=0):
    return jnp.zeros((rows, m.shape[1]), m.dtype).at[at:at + m.shape[0]].set(m)


def _block_diag(w):
    hn, n, _ = w.shape
    out = jnp.zeros((hn * n, hn * n), w.dtype)
    for i in range(hn):
        out = out.at[i * n:(i + 1) * n, i * n:(i + 1) * n].set(w[i])
    return out


def kernel(x, meta, w_in, rw_mu, s5_a_re, s5_a_im, s5_log_dt, s5_b_re, s5_b_im, s5_c_re, s5_c_im, s5_d, s5_w_glu, s5_b_glu, gla_w_gate, gla_b_gate, gla_norm, rw_w0, rw_w2, rw_a0, rw_a2, rw_g2, rw_k_k, rw_k_a, rw_r_k, rw_ln_w, rw_ln_b, rw_v0, rw_v1, rw_v2, lru_conv_w, lru_conv_b, lru_w_a, lru_b_a, lru_w_i, lru_b_i, lru_lam, w_branch, w_merge_gate, b_merge_gate, w_out, norm_mix, norm_ffn, ffn_w_up, ffn_conv_w, ffn_conv_b, ffn_w_down, norm_final):
    bsz, seq, d = x.shape
    assert d == D_MODEL and seq % CHUNK == 0
    lp = FRONT + seq
    tm = _pick_tile(lp, 448)
    tchunk = _pick_tile(lp, 832)
    row = lambda v: v.astype(F32).reshape(1, -1)

    h = jnp.concatenate([jnp.zeros((bsz, PAD, d), x.dtype),
                         jnp.broadcast_to(meta.astype(x.dtype)[None], (bsz, N_META, d)), x], axis=1)
    v_first = None
    for l in range(DEPTH):
        wl = w_in[l]
        o_gla = BRANCH_W
        o_rw = o_gla + 784
        o_lru = o_rw + 896
        hq = GLA_HEADS * GLA_DK
        gla_cols = jnp.concatenate([
            wl[:, o_gla:o_gla + 2 * hq + BRANCH_W],
            wl[:, o_gla + 2 * hq + BRANCH_W + GLA_GATE_RANK:o_rw],
            wl[:, o_gla + 2 * hq + BRANCH_W:o_gla + 2 * hq + BRANCH_W + GLA_GATE_RANK],
            jnp.zeros((d, LANES - GLA_GATE_RANK), wl.dtype)], axis=1)
        rw_cols = wl[:, o_rw:o_lru]
        if l > 0:
            rw_cols = jnp.concatenate([rw_cols, rw_v1[l - 1],
                                       jnp.zeros((d, LANES - RW_V_RANK), wl.dtype)], axis=1)
        wcat = jnp.concatenate([wl[:, 0:BRANCH_W], gla_cols, rw_cols, wl[:, o_lru:]], axis=1).astype(BF16)
        widths = (BRANCH_W, GLA_W, RW_WV if l > 0 else RW_W, LRU_W)
        p5, pg, pr, pu = _proj(h, row(norm_mix[l]), wcat, widths, tm)

        y5 = _s5(p5, _s5_mats(s5_a_re[l], s5_a_im[l], s5_log_dt[l], s5_b_re[l], s5_b_im[l],
                              s5_c_re[l], s5_c_im[l]))
        yg = _gla(pg, _pad_rows(gla_w_gate[l].astype(F32), LANES), row(gla_b_gate[l]), row(gla_norm[l]), tchunk)
        prm = dict(mu=row(rw_mu[l]), w0=row(rw_w0[l]), w2=_pad_rows(rw_w2[l].astype(F32), LANES, 0),
                   a0=row(rw_a0[l]), a2=_pad_rows(rw_a2[l].astype(F32), LANES, RW_W_RANK),
                   g2=_pad_rows(rw_g2[l].astype(F32), LANES, RW_W_RANK + RW_A_RANK),
                   k_k=row(rw_k_k[l]), k_a=row(rw_k_a[l]), r_k=row(rw_r_k[l]),
                   ln_w=row(rw_ln_w[l]), ln_b=row(rw_ln_b[l]))
        if l > 0:
            prm["v0"] = row(rw_v0[l - 1])
            prm["v2"] = _pad_rows(rw_v2[l - 1].astype(F32), LANES)
        yr, v_first = _rwkv(pr, v_first, prm, tchunk)
        wgi = jnp.concatenate([_block_diag(lru_w_a[l]), _block_diag(lru_w_i[l])], axis=1).astype(BF16)
        bgi = jnp.concatenate([lru_b_a[l], lru_b_i[l]]).astype(F32).reshape(1, -1)
        yl = _lru(pu, lru_conv_w[l].astype(F32), row(lru_conv_b[l]), wgi, bgi, row(lru_lam[l]), tchunk)

        h = _merge(h, row(norm_mix[l]), p5, y5, yg, yr, yl, row(s5_d[l]), s5_w_glu[l].astype(BF16),
                   row(s5_b_glu[l]), w_merge_gate[l].astype(BF16),
                   b_merge_gate[l].astype(F32).reshape(4, 1, d), w_branch[l].astype(BF16),
                   w_out[l].astype(BF16), tm)
        h = _ffn(h, row(norm_ffn[l]), ffn_w_up[l].astype(BF16), ffn_conv_w[l].astype(F32),
                 row(ffn_conv_b[l]), ffn_w_down[l].astype(BF16),
                 row(norm_final) if l == DEPTH - 1 else None, tm)
    return h[:, FRONT:]
```

```python
import functools
import math

import numpy as np
import jax
import jax.numpy as jnp
from jax import lax
from jax.experimental import pallas as pl
from jax.experimental.pallas import tpu as pltpu

F32 = jnp.float32
BF16 = jnp.bfloat16

D_MODEL = 1024
DEPTH = 4
N_META = 16
BRANCH_W = 256
S5_GROUP = 16
S5_GROUPS = 16
S5_STATE = 64
S5_BLOCK = 16
GLA_HEADS = 4
GLA_DK = 32
GLA_DV = 64
GLA_GATE_RANK = 16
GLA_TAU = 16.0
CHUNK = 64
RW_HEADS = 4
RW_HEAD = 64
RW_W_RANK = 32
RW_A_RANK = 32
RW_V_RANK = 16
RW_G_RANK = 64
RW_LN_EPS = 64e-5
LRU_C = 8.0
LRU_CONV_W = 4
D_FF = 2816
FFN_CONV_W = 3
RMS_EPS = 1e-6

PAD = (-N_META) % CHUNK
FRONT = PAD + N_META
LANES = 128
SUBLANES = 8
VMEM_LIMIT = 56 * 1024 * 1024

GLA_W = 896
RW_W = 896
RW_WV = 1024
LRU_W = 512


def _mm(a, b):
    return jnp.dot(a.astype(BF16), b.astype(BF16), preferred_element_type=F32)


def _mm_nt(a, b):
    return lax.dot_general(a.astype(BF16), b.astype(BF16), (((1,), (1,)), ((), ())),
                           preferred_element_type=F32)


def _mm_tn(a, b):
    return lax.dot_general(a.astype(BF16), b.astype(BF16), (((0,), (0,)), ((), ())),
                           preferred_element_type=F32)


def _split2(a):
    hi = a.astype(BF16)
    lo = (a - hi.astype(F32)).astype(BF16)
    return hi, lo


def _split3(a):
    hi = a.astype(BF16)
    r1 = a - hi.astype(F32)
    mid = r1.astype(BF16)
    lo = (r1 - mid.astype(F32)).astype(BF16)
    return hi, mid, lo


def _dg(a, b, dims):
    return lax.dot_general(a, b, (dims, ((), ())), preferred_element_type=F32)


def _mm3_dims(a, b, dims):
    ah, al = _split2(a)
    bh, bl = _split2(b)
    return _dg(ah, bh, dims) + (_dg(ah, bl, dims) + _dg(al, bh, dims))


_NN = ((1,), (0,))
_NT = ((1,), (1,))
_TN = ((0,), (0,))


def _mm3(a, b):
    return _mm3_dims(a, b, _NN)


def _mm3_nt(a, b):
    return _mm3_dims(a, b, _NT)


def _mm3_tn(a, b):
    return _mm3_dims(a, b, _TN)


def _mm_exact_rhs(a, b_bf16):
    hi, mid, lo = _split3(a)
    return _dg(hi, b_bf16, _NN) + (_dg(mid, b_bf16, _NN) + _dg(lo, b_bf16, _NN))


def _mm_exact_lhs(a_bf16, b):
    hi, mid, lo = _split3(b)
    return _dg(a_bf16, hi, _NN) + (_dg(a_bf16, mid, _NN) + _dg(a_bf16, lo, _NN))


def _softplus(x):
    return jnp.maximum(x, 0.0) + jnp.log1p(jnp.exp(-jnp.abs(x)))


def _sigmoid(x):
    return jax.nn.sigmoid(x)


def _silu(x):
    return x * jax.nn.sigmoid(x)


def _gelu(x):
    return jax.nn.gelu(x)


def _rmsnorm_rows(x, g):
    ms = jnp.mean(x * x, axis=-1, keepdims=True)
    return x * lax.rsqrt(ms + RMS_EPS) * g


def _iota(shape, dim):
    return lax.broadcasted_iota(jnp.int32, shape, dim)


def _div(x, n):
    return x >> int(math.log2(n))


def _mod(x, n):
    return x & (n - 1)


def _tile_rows4(x):
    return jnp.concatenate([x, x, x, x], axis=0)


def _fold_rows4(z):
    n = z.shape[0] // 4
    return (z[0:n] + z[n:2 * n]) + (z[2 * n:3 * n] + z[3 * n:4 * n])


def _shift_rows(x, s, carry8):
    xs = pltpu.roll(x, s, axis=0)
    top = jnp.where(_iota((SUBLANES, x.shape[1]), 0) < s,
                    pltpu.roll(carry8, s, axis=0), xs[0:SUBLANES])
    return jnp.concatenate([top, xs[SUBLANES:]], axis=0)


def _lockstep(gens):
    gens = list(gens)
    while gens:
        alive = []
        for g in gens:
            try:
                next(g)
                alive.append(g)
            except StopIteration:
                pass
        gens = alive


def _loop_streams(n, rows, chunk_gen):
    def pair(i, carry):
        _lockstep([chunk_gen(r, 2 * i + d) for d in range(2) for r in range(rows)])
        return carry

    if n >= 2:
        lax.fori_loop(0, n // 2, pair, 0)
    if n % 2:
        _lockstep([chunk_gen(r, n - 1) for r in range(rows)])


def _chunk_rows(c):
    start = c * CHUNK
    return pl.ds(start if isinstance(c, int) else pl.multiple_of(start, CHUNK), CHUNK)


def _const_spec(shape):
    nd = len(shape)
    return pl.BlockSpec(shape, lambda *_: (0,) * nd)


def _row_spec(rows, width, nb=1):
    return pl.BlockSpec((nb, rows, width), lambda b, j: (b, j, 0))


STREAM_ROWS = 2


def _stream_rows(b):
    return STREAM_ROWS if b % STREAM_ROWS == 0 else 1


def _params(sem):
    return pltpu.CompilerParams(dimension_semantics=sem, vmem_limit_bytes=VMEM_LIMIT)


def _pick_tile(lp, cap):
    best = CHUNK
    t = CHUNK
    while t <= min(lp, cap):
        if lp % t == 0:
            best = t
        t += CHUNK
    return best


def _proj_kernel(h_ref, g_ref, w_ref, *out_refs, widths):
    xn = _rmsnorm_rows(h_ref[0], g_ref[...]).astype(BF16)
    off = 0
    for o_ref, w in zip(out_refs, widths):
        o_ref[0] = jnp.dot(xn, w_ref[:, off:off + w], preferred_element_type=F32)
        off += w


def _proj(h, g, w, widths, tm):
    b, lp, d = h.shape
    nt = lp // tm
    return pl.pallas_call(
        functools.partial(_proj_kernel, widths=widths),
        grid=(b, nt),
        in_specs=[_row_spec(tm, d), _const_spec((1, d)), _const_spec(w.shape)],
        out_specs=[_row_spec(tm, wd) for wd in widths],
        out_shape=[jax.ShapeDtypeStruct((b, lp, wd), F32) for wd in widths],
        compiler_params=_params(("parallel", "parallel")),
        name="proj",
    )(h, g, w)


def _s5_kernel(u_ref, k_ref, e_ref, c_ref, a_ref, y_ref, e_scr, s_scr, *, nb, bsz):
    u = u_ref[0]
    e_scr[...] = _mm(u, e_ref[0])
    pr = a_ref[0, 0:1, :]
    pi = a_ref[0, 1:2, :]

    def body(k, s):
        rows = pl.ds(pl.multiple_of(k * bsz, bsz), bsz)
        s_scr[rows, :] = s
        return pr * s + pi * pltpu.roll(s, S5_STATE, axis=1) + e_scr[rows, :]

    lax.fori_loop(0, nb, body, jnp.zeros((bsz, 2 * S5_STATE), F32))
    y_ref[0] = (_mm(u, k_ref[0]) + _mm(s_scr[...], c_ref[0])).astype(y_ref.dtype)


def _s5_mats(a_re, a_im, log_dt, b_re, b_im, c_re, c_im):
    lr, li = a_re.astype(F32), a_im.astype(F32)
    dt = jnp.exp(log_dt.astype(F32))[:, None]
    mag = jnp.exp(lr * dt)
    abr, abi = mag * jnp.cos(li * dt), mag * jnp.sin(li * dt)
    den = lr * lr + li * li
    cr = ((abr - 1.0) * lr + abi * li) / den
    ci = (abi * lr - (abr - 1.0) * li) / den
    d = jnp.arange(S5_BLOCK + 1, dtype=F32)[:, None, None]
    pmag = jnp.exp(lr * dt * d)
    p_re, p_im = pmag * jnp.cos(li * dt * d), pmag * jnp.sin(li * dt * d)
    bb_re = cr[..., None] * b_re - ci[..., None] * b_im
    bb_im = cr[..., None] * b_im + ci[..., None] * b_re
    cp_re = c_re[None] * p_re[:, :, None, :] - c_im[None] * p_im[:, :, None, :]
    cp_im = c_re[None] * p_im[:, :, None, :] + c_im[None] * p_re[:, :, None, :]
    hp = lax.Precision.HIGHEST
    kd = (jnp.einsum("dgon,gni->dgoi", cp_re, bb_re, precision=hp)
          - jnp.einsum("dgon,gni->dgoi", cp_im, bb_im, precision=hp))
    t = np.arange(S5_BLOCK)
    lag = t[None, :] - t[:, None]
    ksel = kd[np.clip(lag, 0, None)] * jnp.asarray(lag >= 0, F32)[:, :, None, None, None]
    kmat = ksel.transpose(2, 0, 4, 1, 3).reshape(S5_GROUPS, S5_BLOCK * S5_GROUP, S5_BLOCK * S5_GROUP)
    pe_re, pe_im = p_re[S5_BLOCK - 1 - t], p_im[S5_BLOCK - 1 - t]
    e_re = pe_re[..., None] * bb_re[None] - pe_im[..., None] * bb_im[None]
    e_im = pe_re[..., None] * bb_im[None] + pe_im[..., None] * bb_re[None]
    emat = jnp.concatenate([e_re, e_im], axis=2).transpose(1, 0, 3, 2)
    emat = emat.reshape(S5_GROUPS, S5_BLOCK * S5_GROUP, 2 * S5_STATE)
    cmat = jnp.concatenate([cp_re[1:], -cp_im[1:]], axis=3)
    cmat = cmat.transpose(1, 3, 0, 2).reshape(S5_GROUPS, 2 * S5_STATE, S5_BLOCK * S5_GROUP)
    a16 = jnp.stack([jnp.concatenate([p_re[S5_BLOCK], p_re[S5_BLOCK]], axis=-1),
                     jnp.concatenate([-p_im[S5_BLOCK], p_im[S5_BLOCK]], axis=-1)], axis=1)
    a16 = jnp.pad(a16, ((0, 0), (0, SUBLANES - 2), (0, 0)))
    return kmat, emat, cmat, a16


def _s5(p5, mats):
    kmat, emat, cmat, a16 = mats
    b, lp, _ = p5.shape
    nb = lp // S5_BLOCK
    gw = S5_BLOCK * S5_GROUP
    kmat, emat, cmat = kmat.astype(BF16), emat.astype(BF16), cmat.astype(BF16)
    u = p5.astype(BF16).reshape(b, nb, S5_BLOCK, S5_GROUPS, S5_GROUP).transpose(3, 1, 0, 2, 4)
    u = u.reshape(S5_GROUPS, nb * b, gw)
    gspec = lambda r, c: pl.BlockSpec((1, r, c), lambda g: (g, 0, 0))
    y = pl.pallas_call(
        functools.partial(_s5_kernel, nb=nb, bsz=b),
        grid=(S5_GROUPS,),
        in_specs=[gspec(nb * b, gw), gspec(gw, gw), gspec(gw, 2 * S5_STATE),
                  gspec(2 * S5_STATE, gw), gspec(SUBLANES, 2 * S5_STATE)],
        out_specs=gspec(nb * b, gw),
        out_shape=jax.ShapeDtypeStruct((S5_GROUPS, nb * b, gw), BF16),
        scratch_shapes=[pltpu.VMEM((nb * b, 2 * S5_STATE), F32),
                        pltpu.VMEM((nb * b, 2 * S5_STATE), F32)],
        compiler_params=_params(("parallel",)),
        name="s5",
    )(u, kmat, emat, cmat, a16)
    y = y.reshape(S5_GROUPS, nb, b, S5_BLOCK, S5_GROUP).transpose(2, 1, 3, 0, 4)
    return y.reshape(b, lp, BRANCH_W)


def _gla_kernel(p_ref, wg_ref, bg_ref, nw_ref, y_ref, st_scr, *, cpt, nrows):
    @pl.when(pl.program_id(1) == 0)
    def _():
        st_scr[...] = jnp.zeros_like(st_scr)

    hq = GLA_HEADS * GLA_DK
    hv = GLA_HEADS * GLA_DV
    rows4 = GLA_HEADS * CHUNK
    tri = (_iota((CHUNK, CHUNK), 0) >= _iota((CHUNK, CHUNK), 1)).astype(BF16)
    bd_k = _div(_iota((rows4, hq), 0), CHUNK) == _div(_iota((rows4, hq), 1), GLA_DK)
    causal = _mod(_iota((CHUNK, rows4), 1), CHUNK) <= _iota((CHUNK, rows4), 0)
    bd_v = _div(_iota((rows4, hv), 0), CHUNK) == _div(_iota((rows4, hv), 1), GLA_DV)
    bd_st = _div(_iota((hv, hq), 0), GLA_DV) == _div(_iota((hv, hq), 1), GLA_DK)
    head_ones = (_div(_iota((hv, hv), 0), GLA_DV) == _div(_iota((hv, hv), 1), GLA_DV)).astype(BF16)
    wg = wg_ref[...]
    bg = bg_ref[...]
    nw = nw_ref[...]

    def chunk(bi, c):
        rows = _chunk_rows(c)
        p = p_ref[bi, rows, :]
        q = p[:, 0:hq]
        k = p[:, hq:2 * hq]
        v = p[:, 2 * hq:2 * hq + hv]
        og = p[:, 2 * hq + hv:2 * hq + 2 * hv]
        gd = p[:, 2 * hq + 2 * hv:]
        x = _mm3(gd, wg) + bg
        yield
        g = (jnp.minimum(x, 0.0) - jnp.log1p(jnp.exp(-jnp.abs(x)))) / GLA_TAU
        gc = _mm_exact_lhs(tri, g)
        yield
        gl = gc[CHUNK - 1:CHUNK, :]
        qg = q * (GLA_DK ** -0.5) * jnp.exp(gc)
        kg = k * jnp.exp(-gc)
        kl = k * jnp.exp(gl - gc)
        kg_e = jnp.where(bd_k, _tile_rows4(kg), 0.0)
        att = _mm_nt(qg, kg_e)
        kv = _mm_tn(v, kl)
        yield
        att = jnp.where(causal, att, 0.0)
        v_e = jnp.where(bd_v, _tile_rows4(v), 0.0)
        st = st_scr[bi]
        o = _mm(att, v_e) + _mm_nt(qg, st)
        st_scr[bi] = st * jnp.exp(gl) + jnp.where(bd_st, kv, 0.0)
        yield
        ms = _mm(o * o, head_ones) * (1.0 / GLA_DV)
        yield
        o = o * lax.rsqrt(ms + RMS_EPS) * nw
        y_ref[bi, rows, :] = o * _silu(og)

    _loop_streams(cpt, nrows, chunk)


def _gla(pg, w_gate_pad, b_gate, norm_w, tile):
    b, lp, _ = pg.shape
    cpt = tile // CHUNK
    nrows = _stream_rows(b)
    return pl.pallas_call(
        functools.partial(_gla_kernel, cpt=cpt, nrows=nrows),
        grid=(b // nrows, lp // tile),
        in_specs=[_row_spec(tile, GLA_W, nrows), _const_spec(w_gate_pad.shape),
                  _const_spec(b_gate.shape), _const_spec(norm_w.shape)],
        out_specs=_row_spec(tile, BRANCH_W, nrows),
        out_shape=jax.ShapeDtypeStruct((b, lp, BRANCH_W), F32),
        scratch_shapes=[pltpu.VMEM((nrows, GLA_HEADS * GLA_DV, GLA_HEADS * GLA_DK), F32)],
        compiler_params=_params(("parallel", "arbitrary")),
        name="gla",
    )(pg, w_gate_pad, b_gate, norm_w)


def _rw_kernel(*refs, cpt, nrows, has_vgate):
    if has_vgate:
        (p_ref, vf_ref, mu_ref, w0_ref, w2_ref, a0_ref, a2_ref, g2_ref, kk_ref, ka_ref, rk_ref,
         lnw_ref, lnb_ref, v0_ref, v2_ref, y_ref, carry_scr, h_scr) = refs
    else:
        (p_ref, mu_ref, w0_ref, w2_ref, a0_ref, a2_ref, g2_ref, kk_ref, ka_ref, rk_ref,
         lnw_ref, lnb_ref, y_ref, vf_out_ref, carry_scr, h_scr) = refs

    @pl.when(pl.program_id(1) == 0)
    def _():
        carry_scr[...] = jnp.zeros_like(carry_scr)
        h_scr[...] = jnp.zeros_like(h_scr)

    w = BRANCH_W
    rows4 = RW_HEADS * CHUNK
    r_i = _iota((rows4, w), 0)
    c_i = _iota((rows4, w), 1)
    bd = _div(r_i, CHUNK) == _div(c_i, RW_HEAD)
    eye = (r_i == c_i).astype(F32)
    t_row = _iota((CHUNK, w), 0)
    s_col = _mod(_iota((CHUNK, w), 1), CHUNK)
    strict = s_col < t_row
    incl = s_col <= t_row
    eye_c = (s_col == t_row).astype(F32)
    tri = (_iota((CHUNK, CHUNK), 0) >= _iota((CHUNK, CHUNK), 1)).astype(BF16)
    head_ones = bd.astype(BF16)
    row0 = _iota((CHUNK, RW_W), 0) == 0
    zeros_cw = jnp.zeros((CHUNK, w), F32)
    mu = mu_ref[...]
    inv_n = 1.0 / RW_HEAD

    def expand(x):
        return jnp.where(bd, _tile_rows4(x), 0.0)

    def headsum(x):
        return _mm(x, head_ones)

    def chunk(bi, c):
        rows = _chunk_rows(c)
        p = p_ref[bi, rows, :]
        pf = p[:, 0:RW_W]
        prev = jnp.where(row0, carry_scr[bi, 0:1, :], pltpu.roll(pf, 1, axis=0))
        carry_scr[bi, 0:1, :] = pf[CHUNK - 1:CHUNK, :]
        xs = pf + mu * (prev - pf)
        r = xs[:, 0:w]
        k = xs[:, w:2 * w]
        v = xs[:, 2 * w:3 * w]
        lr = xs[:, 3 * w:]
        wlow = _mm3(jnp.tanh(lr), w2_ref[...])
        alow = _mm(lr, a2_ref[...])
        g = _mm(_sigmoid(lr), g2_ref[...])
        if has_vgate:
            vlow = _mm(p[:, RW_W:], v2_ref[...])
        kk = k * kk_ref[...]
        kss = headsum(kk * kk)
        yield
        w_log = -_softplus(-(w0_ref[...] + wlow)) - 0.5
        lw = -jnp.exp(w_log)
        cum = _mm_exact_lhs(tri, lw)
        ag = _sigmoid(a0_ref[...] + alow)
        if has_vgate:
            v = v + (vf_ref[bi, rows, :] - v) * _sigmoid(v0_ref[...] + vlow)
        else:
            vf_out_ref[bi, rows, :] = v
        kk = kk * lax.rsqrt(jnp.maximum(kss, 1e-24))
        k2 = k * (1.0 + (ag - 1.0) * ka_ref[...])
        bvec = kk * ag
        rk_sum = headsum(r * k2 * rk_ref[...])
        v_e = expand(v)
        yield
        tot = cum[CHUNK - 1:CHUNK, :]
        e_neg = jnp.exp(-cum)
        e_tail = jnp.exp(tot - cum)
        a_n = -kk * jnp.exp(cum - lw)
        r_n = r * jnp.exp(cum)
        lhs = jnp.concatenate([a_n, r_n], axis=0)
        rhs = jnp.concatenate([expand(bvec * e_neg), expand(k2 * e_neg)], axis=0)
        aa = _mm_nt(lhs, rhs)
        yield
        a_ab = jnp.where(strict, aa[0:CHUNK, 0:w], 0.0)
        a_ak = jnp.where(strict, aa[0:CHUNK, w:], 0.0)
        a_rb = jnp.where(incl, aa[CHUNK:, 0:w], 0.0)
        a_rk = jnp.where(incl, aa[CHUNK:, w:], 0.0)
        tinv = eye_c + a_ab
        pw = a_ab
        pw_e = expand(pw)
        akv = _mm(a_ak, v_e)
        rkv = _mm(a_rk, v_e)
        for _ in range(5):
            pw = _mm(pw, pw_e)
            yield
            pw_e = expand(pw)
            tinv = tinv + _mm(tinv, pw_e)
        yield
        x_n = _mm(tinv, jnp.concatenate([expand(a_n), expand(akv)], axis=1))
        yield
        yq = _mm(a_rb, jnp.concatenate([expand(x_n[:, 0:w]), expand(x_n[:, w:])], axis=1))
        mn = _mm_tn(jnp.concatenate([bvec * e_tail, k2 * e_tail], axis=0),
                    jnp.concatenate([x_n, jnp.concatenate([zeros_cw, v], axis=1)], axis=0))
        yield
        q_n = r_n + yq[:, 0:w]
        ol_n = yq[:, w:] + rkv
        m_mat = jnp.where(bd, mn[:, 0:w], 0.0) + eye * jnp.exp(tot)
        n_mat = jnp.where(bd, mn[:, w:], 0.0)
        h = h_scr[bi]
        o = ol_n + _mm(q_n, h)
        h_scr[bi] = _mm(m_mat, h) + n_mat
        yield
        mean = headsum(o) * inv_n
        yield
        dlt = o - mean
        var = headsum(dlt * dlt) * inv_n
        yield
        o = dlt * lax.rsqrt(var + RW_LN_EPS) * lnw_ref[...] + lnb_ref[...]
        y_ref[bi, rows, :] = (o + rk_sum * v) * g

    _loop_streams(cpt, nrows, chunk)


def _rwkv(pr, v_first, prm, tile):
    b, lp, pw = pr.shape
    has_vgate = v_first is not None
    cpt = tile // CHUNK
    vec = _const_spec((1, BRANCH_W))
    mat = _const_spec((LANES, BRANCH_W))
    nrows = _stream_rows(b)
    ins = [pr]
    specs = [_row_spec(tile, pw, nrows)]
    if has_vgate:
        ins.append(v_first)
        specs.append(_row_spec(tile, BRANCH_W, nrows))
    ins += [prm["mu"], prm["w0"], prm["w2"], prm["a0"], prm["a2"], prm["g2"], prm["k_k"], prm["k_a"],
            prm["r_k"], prm["ln_w"], prm["ln_b"]]
    specs += [_const_spec((1, RW_W)), vec, mat, vec, mat, mat, vec, vec, vec, vec, vec]
    out_shape = [jax.ShapeDtypeStruct((b, lp, BRANCH_W), F32)]
    out_specs = [_row_spec(tile, BRANCH_W, nrows)]
    if has_vgate:
        ins += [prm["v0"], prm["v2"]]
        specs += [vec, mat]
    else:
        out_shape.append(jax.ShapeDtypeStruct((b, lp, BRANCH_W), F32))
        out_specs.append(_row_spec(tile, BRANCH_W, nrows))
    outs = pl.pallas_call(
        functools.partial(_rw_kernel, cpt=cpt, nrows=nrows, has_vgate=has_vgate),
        grid=(b // nrows, lp // tile),
        in_specs=specs,
        out_specs=out_specs,
        out_shape=out_shape,
        scratch_shapes=[pltpu.VMEM((nrows, SUBLANES, RW_W), F32),
                        pltpu.VMEM((nrows, RW_HEADS * RW_HEAD, BRANCH_W), F32)],
        compiler_params=_params(("parallel", "arbitrary")),
        name="rwkv7",
    )(*ins)
    if has_vgate:
        return outs[0], v_first
    return outs[0], outs[1]


def _lru_kernel(p_ref, cw_ref, cb_ref, wg_ref, bg_ref, lam_ref, y_ref, x_scr, h_scr, *, tl):
    j = pl.program_id(1)

    @pl.when(j == 0)
    def _():
        x_scr[...] = jnp.zeros_like(x_scr)
        h_scr[...] = jnp.zeros_like(h_scr)

    w = BRANCH_W
    p = p_ref[0]
    xr = p[:, 0:w]
    gate = p[:, w:]
    carry8 = x_scr[...]
    xc = cb_ref[...] + cw_ref[LRU_CONV_W - 1:LRU_CONV_W, :] * xr
    for s in range(1, LRU_CONV_W):
        xc = xc + cw_ref[LRU_CONV_W - 1 - s:LRU_CONV_W - s, :] * _shift_rows(xr, s, carry8)
    x_scr[...] = xr[tl - SUBLANES:tl, :]
    gates = _mm(xc, wg_ref[...]) + bg_ref[...]
    rg = _sigmoid(gates[:, 0:w])
    ig = _sigmoid(gates[:, w:])
    log_a = -LRU_C * rg * _softplus(-lam_ref[...])
    a = jnp.exp(log_a)
    mult = jnp.sqrt(1.0 - jnp.exp(2.0 * log_a))
    ridx = j * tl + _iota((tl, w), 0)
    mult = jnp.where(ridx == PAD, 1.0, mult)
    bterm = jnp.where(ridx >= PAD, mult * ig * xc, 0.0)
    t_i = _iota((tl, w), 0)
    d = 1
    while d < tl:
        keep = t_i >= d
        bterm = bterm + a * jnp.where(keep, pltpu.roll(bterm, d, axis=0), 0.0)
        a = a * jnp.where(keep, pltpu.roll(a, d, axis=0), 1.0)
        d *= 2
    h = bterm + a * h_scr[0:1, :]
    h_scr[0:1, :] = h[tl - 1:tl, :]
    y_ref[0] = h * _gelu(gate)


def _lru(pl_, cw, cb, wg, bg, lam, tile):
    b, lp, _ = pl_.shape
    vec = _const_spec((1, BRANCH_W))
    return pl.pallas_call(
        functools.partial(_lru_kernel, tl=tile),
        grid=(b, lp // tile),
        in_specs=[_row_spec(tile, LRU_W), _const_spec(cw.shape), vec, _const_spec(wg.shape),
                  _const_spec(bg.shape), vec],
        out_specs=_row_spec(tile, BRANCH_W),
        out_shape=jax.ShapeDtypeStruct((b, lp, BRANCH_W), F32),
        scratch_shapes=[pltpu.VMEM((SUBLANES, BRANCH_W), F32), pltpu.VMEM((SUBLANES, BRANCH_W), F32)],
        compiler_params=_params(("parallel", "arbitrary")),
        name="rglru",
    )(pl_, cw, cb, wg, bg, lam)


def _merge_kernel(h_ref, g_ref, u5_ref, y5_ref, yg_ref, yr_ref, yl_ref, d5_ref, wglu_ref, bglu_ref,
                  wgate_ref, bgate_ref, wbr_ref, wout_ref, o_ref, *, tm):
    x = h_ref[0]
    xn = _rmsnorm_rows(x, g_ref[...]).astype(BF16)
    ys = _gelu(y5_ref[0] + d5_ref[...] * u5_ref[0])
    ys = ys * _sigmoid(_mm(ys, wglu_ref[...]) + bglu_ref[...])
    branches = (ys, yg_ref[0], yr_ref[0], yl_ref[0])
    merged = None
    for i, yb in enumerate(branches):
        gate = _sigmoid(jnp.dot(xn, wgate_ref[i], preferred_element_type=F32) + bgate_ref[i])
        term = gate * _mm(yb, wbr_ref[i])
        merged = term if merged is None else merged + term
    upd = _mm(merged, wout_ref[...])
    ridx = pl.program_id(1) * tm + _iota((tm, 1), 0)
    o_ref[0] = x + jnp.where(ridx >= PAD, upd, 0.0)


def _merge(h, g, u5, y5, yg, yr, yl, d5, wglu, bglu, wgate, bgate, wbr, wout, tm):
    b, lp, d = h.shape
    ybs = _row_spec(tm, BRANCH_W)
    return pl.pallas_call(
        functools.partial(_merge_kernel, tm=tm),
        grid=(b, lp // tm),
        in_specs=[_row_spec(tm, d), _const_spec((1, d)), ybs, ybs, ybs, ybs, ybs,
                  _const_spec((1, BRANCH_W)), _const_spec(wglu.shape), _const_spec((1, BRANCH_W)),
                  _const_spec(wgate.shape), _const_spec(bgate.shape), _const_spec(wbr.shape),
                  _const_spec(wout.shape)],
        out_specs=_row_spec(tm, d),
        out_shape=jax.ShapeDtypeStruct((b, lp, d), F32),
        compiler_params=_params(("parallel", "parallel")),
        name="merge",
    )(h, g, u5, y5, yg, yr, yl, d5, wglu, bglu, wgate, bgate, wbr, wout)


FFN_COLS = 256


def _ffn_kernel(*refs, tm, final):
    if final:
        h_ref, g_ref, wup_ref, cw_ref, cb_ref, wdn_ref, gf_ref, o_ref, g_scr = refs
    else:
        h_ref, g_ref, wup_ref, cw_ref, cb_ref, wdn_ref, o_ref, g_scr = refs
    j = pl.program_id(1)

    @pl.when(j == 0)
    def _():
        g_scr[...] = jnp.zeros_like(g_scr)

    x = h_ref[0]
    xn = _rmsnorm_rows(x, g_ref[...]).astype(BF16)
    acc = jnp.zeros((tm, D_MODEL), F32)
    for c in range(D_FF // FFN_COLS):
        lo = c * FFN_COLS
        gate = jnp.dot(xn, wup_ref[:, lo:lo + FFN_COLS], preferred_element_type=F32)
        up = jnp.dot(xn, wup_ref[:, D_FF + lo:D_FF + lo + FFN_COLS], preferred_element_type=F32)
        carry8 = g_scr[:, lo:lo + FFN_COLS]
        g_scr[:, lo:lo + FFN_COLS] = gate[tm - SUBLANES:tm, :]
        gc = cb_ref[:, lo:lo + FFN_COLS] + cw_ref[FFN_CONV_W - 1:FFN_CONV_W, lo:lo + FFN_COLS] * gate
        for s in range(1, FFN_CONV_W):
            gc = gc + (cw_ref[FFN_CONV_W - 1 - s:FFN_CONV_W - s, lo:lo + FFN_COLS]
                       * _shift_rows(gate, s, carry8))
        acc = acc + _mm(_silu(gc) * up, wdn_ref[lo:lo + FFN_COLS, :])
    ridx = j * tm + _iota((tm, 1), 0)
    out = x + jnp.where(ridx >= PAD, acc, 0.0)
    if final:
        out = _rmsnorm_rows(out, gf_ref[...])
    o_ref[0] = out


def _ffn(h, g, wup, cw, cb, wdn, gf, tm):
    b, lp, d = h.shape
    final = gf is not None
    ins = [h, g, wup, cw, cb, wdn]
    specs = [_row_spec(tm, d), _const_spec((1, d)), _const_spec(wup.shape), _const_spec(cw.shape),
             _const_spec(cb.shape), _const_spec(wdn.shape)]
    if final:
        ins.append(gf)
        specs.append(_const_spec((1, d)))
    return pl.pallas_call(
        functools.partial(_ffn_kernel, tm=tm, final=final),
        grid=(b, lp // tm),
        in_specs=specs,
        out_specs=_row_spec(tm, d),
        out_shape=jax.ShapeDtypeStruct((b, lp, d), F32),
        scratch_shapes=[pltpu.VMEM((SUBLANES, D_FF), F32)],
        compiler_params=_params(("parallel", "arbitrary")),
        name="ffn",
    )(*ins)


def _pad_rows(m, rows, at=0):
    return jnp.zeros((rows, m.shape[1]), m.dtype).at[at:at + m.shape[0]].set(m)


def _block_diag(w):
    hn, n, _ = w.shape
    out = jnp.zeros((hn * n, hn * n), w.dtype)
    for i in range(hn):
        out = out.at[i * n:(i + 1) * n, i * n:(i + 1) * n].set(w[i])
    return out


def kernel(x, meta, w_in, rw_mu, s5_a_re, s5_a_im, s5_log_dt, s5_b_re, s5_b_im, s5_c_re, s5_c_im, s5_d, s5_w_glu, s5_b_glu, gla_w_gate, gla_b_gate, gla_norm, rw_w0, rw_w2, rw_a0, rw_a2, rw_g2, rw_k_k, rw_k_a, rw_r_k, rw_ln_w, rw_ln_b, rw_v0, rw_v1, rw_v2, lru_conv_w, lru_conv_b, lru_w_a, lru_b_a, lru_w_i, lru_b_i, lru_lam, w_branch, w_merge_gate, b_merge_gate, w_out, norm_mix, norm_ffn, ffn_w_up, ffn_conv_w, ffn_conv_b, ffn_w_down, norm_final):
    bsz, seq, d = x.shape
    assert d == D_MODEL and seq % CHUNK == 0
    lp = FRONT + seq
    tm = _pick_tile(lp, 448)
    tchunk = _pick_tile(lp, 832)
    row = lambda v: v.astype(F32).reshape(1, -1)

    h = jnp.concatenate([jnp.zeros((bsz, PAD, d), x.dtype),
                         jnp.broadcast_to(meta.astype(x.dtype)[None], (bsz, N_META, d)), x], axis=1)
    v_first = None
    for l in range(DEPTH):
        wl = w_in[l]
        o_gla = BRANCH_W
        o_rw = o_gla + 784
        o_lru = o_rw + 896
        hq = GLA_HEADS * GLA_DK
        gla_cols = jnp.concatenate([
            wl[:, o_gla:o_gla + 2 * hq + BRANCH_W],
            wl[:, o_gla + 2 * hq + BRANCH_W + GLA_GATE_RANK:o_rw],
            wl[:, o_gla + 2 * hq + BRANCH_W:o_gla + 2 * hq + BRANCH_W + GLA_GATE_RANK],
            jnp.zeros((d, LANES - GLA_GATE_RANK), wl.dtype)], axis=1)
        rw_cols = wl[:, o_rw:o_lru]
        if l > 0:
            rw_cols = jnp.concatenate([rw_cols, rw_v1[l - 1],
                                       jnp.zeros((d, LANES - RW_V_RANK), wl.dtype)], axis=1)
        wcat = jnp.concatenate([wl[:, 0:BRANCH_W], gla_cols, rw_cols, wl[:, o_lru:]], axis=1).astype(BF16)
        widths = (BRANCH_W, GLA_W, RW_WV if l > 0 else RW_W, LRU_W)
        p5, pg, pr, pu = _proj(h, row(norm_mix[l]), wcat, widths, tm)

        y5 = _s5(p5, _s5_mats(s5_a_re[l], s5_a_im[l], s5_log_dt[l], s5_b_re[l], s5_b_im[l],
                              s5_c_re[l], s5_c_im[l]))
        yg = _gla(pg, _pad_rows(gla_w_gate[l].astype(F32), LANES), row(gla_b_gate[l]), row(gla_norm[l]), tchunk)
        prm = dict(mu=row(rw_mu[l]), w0=row(rw_w0[l]), w2=_pad_rows(rw_w2[l].astype(F32), LANES, 0),
                   a0=row(rw_a0[l]), a2=_pad_rows(rw_a2[l].astype(F32), LANES, RW_W_RANK),
                   g2=_pad_rows(rw_g2[l].astype(F32), LANES, RW_W_RANK + RW_A_RANK),
                   k_k=row(rw_k_k[l]), k_a=row(rw_k_a[l]), r_k=row(rw_r_k[l]),
                   ln_w=row(rw_ln_w[l]), ln_b=row(rw_ln_b[l]))
        if l > 0:
            prm["v0"] = row(rw_v0[l - 1])
            prm["v2"] = _pad_rows(rw_v2[l - 1].astype(F32), LANES)
        yr, v_first = _rwkv(pr, v_first, prm, tchunk)
        wgi = jnp.concatenate([_block_diag(lru_w_a[l]), _block_diag(lru_w_i[l])], axis=1).astype(BF16)
        bgi = jnp.concatenate([lru_b_a[l], lru_b_i[l]]).astype(F32).reshape(1, -1)
        yl = _lru(pu, lru_conv_w[l].astype(F32), row(lru_conv_b[l]), wgi, bgi, row(lru_lam[l]), tchunk)

        h = _merge(h, row(norm_mix[l]), p5, y5, yg, yr, yl, row(s5_d[l]), s5_w_glu[l].astype(BF16),
                   row(s5_b_glu[l]), w_merge_gate[l].astype(BF16),
                   b_merge_gate[l].astype(F32).reshape(4, 1, d), w_branch[l].astype(BF16),
                   w_out[l].astype(BF16), tm)
        h = _ffn(h, row(norm_ffn[l]), ffn_w_up[l].astype(BF16), ffn_conv_w[l].astype(F32),
                 row(ffn_conv_b[l]), ffn_w_down[l].astype(BF16),
                 row(norm_final) if l == DEPTH - 1 else None, tm)
    return h[:, FRONT:]
```

```python
import functools
import math

import numpy as np
import jax
import jax.numpy as jnp
from jax import lax
from jax.experimental import pallas as pl
from jax.experimental.pallas import tpu as pltpu

F32 = jnp.float32
BF16 = jnp.bfloat16

D_MODEL = 1024
DEPTH = 4
N_META = 16
BRANCH_W = 256
S5_GROUP = 16
S5_GROUPS = 16
S5_STATE = 64
S5_BLOCK = 16
GLA_HEADS = 4
GLA_DK = 32
GLA_DV = 64
GLA_GATE_RANK = 16
GLA_TAU = 16.0
CHUNK = 64
RW_HEADS = 4
RW_HEAD = 64
RW_W_RANK = 32
RW_A_RANK = 32
RW_V_RANK = 16
RW_G_RANK = 64
RW_LN_EPS = 64e-5
LRU_C = 8.0
LRU_CONV_W = 4
D_FF = 2816
FFN_CONV_W = 3
RMS_EPS = 1e-6

PAD = (-N_META) % CHUNK
FRONT = PAD + N_META
LANES = 128
SUBLANES = 8
VMEM_LIMIT = 56 * 1024 * 1024

GLA_W = 896
RW_W = 896
RW_WV = 1024
LRU_W = 512


def _mm(a, b):
    return jnp.dot(a.astype(BF16), b.astype(BF16), preferred_element_type=F32)


def _mm_nt(a, b):
    return lax.dot_general(a.astype(BF16), b.astype(BF16), (((1,), (1,)), ((), ())),
                           preferred_element_type=F32)


def _mm_tn(a, b):
    return lax.dot_general(a.astype(BF16), b.astype(BF16), (((0,), (0,)), ((), ())),
                           preferred_element_type=F32)


def _split2(a):
    hi = a.astype(BF16)
    lo = (a - hi.astype(F32)).astype(BF16)
    return hi, lo


def _split3(a):
    hi = a.astype(BF16)
    r1 = a - hi.astype(F32)
    mid = r1.astype(BF16)
    lo = (r1 - mid.astype(F32)).astype(BF16)
    return hi, mid, lo


def _dg(a, b, dims):
    return lax.dot_general(a, b, (dims, ((), ())), preferred_element_type=F32)


def _mm3_dims(a, b, dims):
    ah, al = _split2(a)
    bh, bl = _split2(b)
    return _dg(ah, bh, dims) + (_dg(ah, bl, dims) + _dg(al, bh, dims))


_NN = ((1,), (0,))
_NT = ((1,), (1,))
_TN = ((0,), (0,))


def _mm3(a, b):
    return _mm3_dims(a, b, _NN)


def _mm3_nt(a, b):
    return _mm3_dims(a, b, _NT)


def _mm3_tn(a, b):
    return _mm3_dims(a, b, _TN)


def _mm_exact_rhs(a, b_bf16):
    hi, mid, lo = _split3(a)
    return _dg(hi, b_bf16, _NN) + (_dg(mid, b_bf16, _NN) + _dg(lo, b_bf16, _NN))


def _mm_exact_lhs(a_bf16, b):
    hi, mid, lo = _split3(b)
    return _dg(a_bf16, hi, _NN) + (_dg(a_bf16, mid, _NN) + _dg(a_bf16, lo, _NN))


def _softplus(x):
    return jnp.maximum(x, 0.0) + jnp.log1p(jnp.exp(-jnp.abs(x)))


def _sigmoid(x):
    return jax.nn.sigmoid(x)


def _silu(x):
    return x * jax.nn.sigmoid(x)


def _gelu(x):
    return jax.nn.gelu(x)


def _rmsnorm_rows(x, g):
    ms = jnp.mean(x * x, axis=-1, keepdims=True)
    return x * lax.rsqrt(ms + RMS_EPS) * g


def _iota(shape, dim):
    return lax.broadcasted_iota(jnp.int32, shape, dim)


def _div(x, n):
    return x >> int(math.log2(n))


def _mod(x, n):
    return x & (n - 1)


def _tile_rows4(x):
    return jnp.concatenate([x, x, x, x], axis=0)


def _fold_rows4(z):
    n = z.shape[0] // 4
    return (z[0:n] + z[n:2 * n]) + (z[2 * n:3 * n] + z[3 * n:4 * n])


def _shift_rows(x, s, carry8):
    xs = pltpu.roll(x, s, axis=0)
    top = jnp.where(_iota((SUBLANES, x.shape[1]), 0) < s,
                    pltpu.roll(carry8, s, axis=0), xs[0:SUBLANES])
    return jnp.concatenate([top, xs[SUBLANES:]], axis=0)


def _lockstep(gens):
    gens = list(gens)
    while gens:
        alive = []
        for g in gens:
            try:
                next(g)
                alive.append(g)
            except StopIteration:
                pass
        gens = alive


def _loop_streams(n, rows, chunk_gen, depth):
    def trip(i, carry):
        _lockstep([chunk_gen(r, depth * i + d) for d in range(depth) for r in range(rows)])
        return carry

    if n >= depth:
        lax.fori_loop(0, n // depth, trip, 0)
    tail = n % depth
    if tail:
        _lockstep([chunk_gen(r, n - tail + d) for d in range(tail) for r in range(rows)])


def _chunk_rows(c):
    start = c * CHUNK
    return pl.ds(start if isinstance(c, int) else pl.multiple_of(start, CHUNK), CHUNK)


def _const_spec(shape):
    nd = len(shape)
    return pl.BlockSpec(shape, lambda *_: (0,) * nd, pipeline_mode=pl.Buffered(1))


def _row_spec(rows, width, nb=1):
    return pl.BlockSpec((nb, rows, width), lambda b, j: (b, j, 0))


STREAM_ROWS = 2
STREAM_DEPTH = 4


def _stream_rows(b):
    return STREAM_ROWS if b % STREAM_ROWS == 0 else 1


def _params(sem):
    return pltpu.CompilerParams(dimension_semantics=sem, vmem_limit_bytes=VMEM_LIMIT)


def _pick_tile(lp, cap, step=CHUNK):
    best = step
    t = step
    while t <= min(lp, cap):
        if lp % t == 0:
            best = t
        t += step
    return best


def _proj_kernel(h_ref, g_ref, w_ref, *out_refs, widths):
    xn = _rmsnorm_rows(h_ref[0], g_ref[...]).astype(BF16)
    off = 0
    for o_ref, w in zip(out_refs, widths):
        o_ref[0] = jnp.dot(xn, w_ref[:, off:off + w], preferred_element_type=F32)
        off += w


def _proj(h, g, w, widths, tm):
    b, lp, d = h.shape
    nt = lp // tm
    return pl.pallas_call(
        functools.partial(_proj_kernel, widths=widths),
        grid=(b, nt),
        in_specs=[_row_spec(tm, d), _const_spec((1, d)), _const_spec(w.shape)],
        out_specs=[_row_spec(tm, wd) for wd in widths],
        out_shape=[jax.ShapeDtypeStruct((b, lp, wd), F32) for wd in widths],
        compiler_params=_params(("parallel", "parallel")),
        name="proj",
    )(h, g, w)


def _cmul(pr, pi, zr, zi):
    return pr * zr - pi * zi, pr * zi + pi * zr


def _s5_kernel(u_ref, kd_ref, bb_ref, cc_ref, pe_ref, pc_ref, a_ref, y_ref,
               io_scr, xs_scr, ys_scr, e_scr, sp_scr, *, nb, nbp):
    ns = S5_GROUPS * S5_STATE
    pad = nbp - nb
    halves = BRANCH_W // LANES
    for hf in range(halves):
        io_scr[hf] = u_ref[0, :, hf * LANES:(hf + 1) * LANES]
    for s in range(S5_BLOCK):
        for hf in range(halves):
            xs_scr[s, 0:nb, hf * LANES:(hf + 1) * LANES] = (
                io_scr[hf, pl.ds(s, nb, stride=S5_BLOCK), :].astype(BF16))
        if pad:
            xs_scr[s, nb:nbp, :] = jnp.zeros((pad, BRANCH_W), BF16)
    if pad:
        sp_scr[nb:nbp, :] = jnp.zeros((pad, 2 * ns), F32)

    e_re = None
    for s in range(S5_BLOCK):
        z = jnp.dot(xs_scr[s], bb_ref[...], preferred_element_type=F32)
        zr, zi = _cmul(pe_ref[s:s + 1, 0:ns], pe_ref[s:s + 1, ns:], z[:, 0:ns], z[:, ns:])
        e_re, e_im = (zr, zi) if e_re is None else (e_re + zr, e_im + zi)
    e_scr[:, 0:ns] = e_re
    e_scr[:, ns:] = e_im

    ar = a_ref[0:1, 0:ns]
    ai = a_ref[0:1, ns:]

    def step(k, carry):
        sr, si = carry
        sp_scr[pl.ds(k, 1), 0:ns] = sr
        sp_scr[pl.ds(k, 1), ns:] = si
        nr, ni = _cmul(ar, ai, sr, si)
        return nr + e_scr[pl.ds(k, 1), 0:ns], ni + e_scr[pl.ds(k, 1), ns:]

    zero = jnp.zeros((1, ns), F32)
    lax.fori_loop(0, nb, step, (zero, zero))

    for d in range(S5_BLOCK):
        m = S5_BLOCK - d
        prod = jnp.dot(xs_scr[0:m].reshape(m * nbp, BRANCH_W), kd_ref[d], preferred_element_type=F32)
        prod = prod.reshape(m, nbp, BRANCH_W)
        if d == 0:
            ys_scr[...] = prod
        else:
            ys_scr[d:S5_BLOCK] = ys_scr[d:S5_BLOCK] + prod

    spr = sp_scr[:, 0:ns]
    spi = sp_scr[:, ns:]
    for t in range(S5_BLOCK):
        gr, gi = _cmul(pc_ref[t:t + 1, 0:ns], pc_ref[t:t + 1, ns:], spr, spi)
        g = jnp.concatenate([gr, gi], axis=1).astype(BF16)
        yt = ys_scr[t] + jnp.dot(g, cc_ref[...], preferred_element_type=F32)
        for hf in range(halves):
            io_scr[hf, pl.ds(t, nb, stride=S5_BLOCK), :] = yt[0:nb, hf * LANES:(hf + 1) * LANES]
    for hf in range(halves):
        y_ref[0, :, hf * LANES:(hf + 1) * LANES] = io_scr[hf]


def _s5_mats(a_re, a_im, log_dt, b_re, b_im, c_re, c_im):
    lr, li = a_re.astype(F32), a_im.astype(F32)
    dt = jnp.exp(log_dt.astype(F32))[:, None]
    mag = jnp.exp(lr * dt)
    abr, abi = mag * jnp.cos(li * dt), mag * jnp.sin(li * dt)
    den = lr * lr + li * li
    cr = ((abr - 1.0) * lr + abi * li) / den
    ci = (abi * lr - (abr - 1.0) * li) / den
    d = jnp.arange(S5_BLOCK + 1, dtype=F32)[:, None, None]
    pmag = jnp.exp(lr * dt * d)
    p_re, p_im = pmag * jnp.cos(li * dt * d), pmag * jnp.sin(li * dt * d)
    bb_re = cr[..., None] * b_re - ci[..., None] * b_im
    bb_im = cr[..., None] * b_im + ci[..., None] * b_re
    cp_re = c_re[None] * p_re[:, :, None, :] - c_im[None] * p_im[:, :, None, :]
    cp_im = c_re[None] * p_im[:, :, None, :] + c_im[None] * p_re[:, :, None, :]
    hp = lax.Precision.HIGHEST
    kd = (jnp.einsum("dgon,gni->dgoi", cp_re, bb_re, precision=hp)
          - jnp.einsum("dgon,gni->dgoi", cp_im, bb_im, precision=hp))
    eye_g = jnp.eye(S5_GROUPS, dtype=F32)
    kmat = jnp.einsum("dgoi,gq->dgiqo", kd[:S5_BLOCK], eye_g).reshape(S5_BLOCK, BRANCH_W, BRANCH_W)
    ns = S5_GROUPS * S5_STATE
    bmat = jnp.concatenate([jnp.einsum("gni,gq->giqn", bb_re, eye_g).reshape(BRANCH_W, ns),
                            jnp.einsum("gni,gq->giqn", bb_im, eye_g).reshape(BRANCH_W, ns)], axis=1)
    cmat = jnp.concatenate([jnp.einsum("gon,gq->gnqo", c_re.astype(F32), eye_g).reshape(ns, BRANCH_W),
                            jnp.einsum("gon,gq->gnqo", -c_im.astype(F32), eye_g).reshape(ns, BRANCH_W)], axis=0)
    flat = lambda z: z.reshape(z.shape[0], ns)
    t = np.arange(S5_BLOCK)
    pe = jnp.concatenate([flat(p_re[S5_BLOCK - 1 - t]), flat(p_im[S5_BLOCK - 1 - t])], axis=1)
    pc = jnp.concatenate([flat(p_re[1:]), flat(p_im[1:])], axis=1)
    a16 = jnp.concatenate([flat(p_re[S5_BLOCK:]), flat(p_im[S5_BLOCK:])], axis=1)
    a16 = jnp.pad(a16, ((0, SUBLANES - 1), (0, 0)))
    return kmat.astype(BF16), bmat.astype(BF16), cmat.astype(BF16), pe, pc, a16


def _s5(p5, mats):
    kmat, bmat, cmat, pe, pc, a16 = mats
    b, lp, _ = p5.shape
    nb = lp // S5_BLOCK
    nbp = -(-nb // 16) * 16
    ns2 = 2 * S5_GROUPS * S5_STATE
    return pl.pallas_call(
        functools.partial(_s5_kernel, nb=nb, nbp=nbp),
        grid=(b,),
        in_specs=[pl.BlockSpec((1, lp, BRANCH_W), lambda i: (i, 0, 0)), _const_spec(kmat.shape),
                  _const_spec(bmat.shape), _const_spec(cmat.shape), _const_spec(pe.shape),
                  _const_spec(pc.shape), _const_spec(a16.shape)],
        out_specs=pl.BlockSpec((1, lp, BRANCH_W), lambda i: (i, 0, 0)),
        out_shape=jax.ShapeDtypeStruct((b, lp, BRANCH_W), F32),
        scratch_shapes=[pltpu.VMEM((BRANCH_W // LANES, lp, LANES), F32),
                        pltpu.VMEM((S5_BLOCK, nbp, BRANCH_W), BF16),
                        pltpu.VMEM((S5_BLOCK, nbp, BRANCH_W), F32),
                        pltpu.VMEM((nbp, ns2), F32),
                        pltpu.VMEM((nbp, ns2), F32)],
        compiler_params=_params(("parallel",)),
        name="s5",
    )(p5, kmat, bmat, cmat, pe, pc, a16)


def _gla_kernel(p_ref, wg_ref, bg_ref, nw_ref, y_ref, st_scr, *, cpt, nrows):
    @pl.when(pl.program_id(1) == 0)
    def _():
        st_scr[...] = jnp.zeros_like(st_scr)

    hq = GLA_HEADS * GLA_DK
    hv = GLA_HEADS * GLA_DV
    rows4 = GLA_HEADS * CHUNK
    tri = (_iota((CHUNK, CHUNK), 0) >= _iota((CHUNK, CHUNK), 1)).astype(BF16)
    bd_k = _div(_iota((rows4, hq), 0), CHUNK) == _div(_iota((rows4, hq), 1), GLA_DK)
    causal = _mod(_iota((CHUNK, rows4), 1), CHUNK) <= _iota((CHUNK, rows4), 0)
    bd_v = _div(_iota((rows4, hv), 0), CHUNK) == _div(_iota((rows4, hv), 1), GLA_DV)
    bd_st = _div(_iota((hv, hq), 0), GLA_DV) == _div(_iota((hv, hq), 1), GLA_DK)
    head_ones = (_div(_iota((hv, hv), 0), GLA_DV) == _div(_iota((hv, hv), 1), GLA_DV)).astype(BF16)
    wg = wg_ref[...]
    bg = bg_ref[...]
    nw = nw_ref[...]

    def chunk(bi, c):
        rows = _chunk_rows(c)
        p = p_ref[bi, rows, :]
        q = p[:, 0:hq]
        k = p[:, hq:2 * hq]
        v = p[:, 2 * hq:2 * hq + hv]
        og = p[:, 2 * hq + hv:2 * hq + 2 * hv]
        gd = p[:, 2 * hq + 2 * hv:]
        x = _mm3(gd, wg) + bg
        yield
        g = (jnp.minimum(x, 0.0) - jnp.log1p(jnp.exp(-jnp.abs(x)))) / GLA_TAU
        gc = _mm_exact_lhs(tri, g)
        yield
        gl = gc[CHUNK - 1:CHUNK, :]
        qg = q * (GLA_DK ** -0.5) * jnp.exp(gc)
        kg = k * jnp.exp(-gc)
        kl = k * jnp.exp(gl - gc)
        kg_e = jnp.where(bd_k, _tile_rows4(kg), 0.0)
        att = _mm_nt(qg, kg_e)
        kv = _mm_tn(v, kl)
        yield
        att = jnp.where(causal, att, 0.0)
        v_e = jnp.where(bd_v, _tile_rows4(v), 0.0)
        st = st_scr[bi]
        o = _mm(att, v_e) + _mm_nt(qg, st)
        st_scr[bi] = st * jnp.exp(gl) + jnp.where(bd_st, kv, 0.0)
        yield
        ms = _mm(o * o, head_ones) * (1.0 / GLA_DV)
        yield
        o = o * lax.rsqrt(ms + RMS_EPS) * nw
        y_ref[bi, rows, :] = o * _silu(og)

    _loop_streams(cpt, nrows, chunk, STREAM_DEPTH)


def _gla(pg, w_gate_pad, b_gate, norm_w, tile):
    b, lp, _ = pg.shape
    cpt = tile // CHUNK
    nrows = _stream_rows(b)
    return pl.pallas_call(
        functools.partial(_gla_kernel, cpt=cpt, nrows=nrows),
        grid=(b // nrows, lp // tile),
        in_specs=[_row_spec(tile, GLA_W, nrows), _const_spec(w_gate_pad.shape),
                  _const_spec(b_gate.shape), _const_spec(norm_w.shape)],
        out_specs=_row_spec(tile, BRANCH_W, nrows),
        out_shape=jax.ShapeDtypeStruct((b, lp, BRANCH_W), F32),
        scratch_shapes=[pltpu.VMEM((nrows, GLA_HEADS * GLA_DV, GLA_HEADS * GLA_DK), F32)],
        compiler_params=_params(("parallel", "arbitrary")),
        name="gla",
    )(pg, w_gate_pad, b_gate, norm_w)


def _rw_kernel(*refs, cpt, nrows, has_vgate):
    if has_vgate:
        (p_ref, vf_ref, mu_ref, w0_ref, w2_ref, a0_ref, a2_ref, g2_ref, kk_ref, ka_ref, rk_ref,
         lnw_ref, lnb_ref, v0_ref, v2_ref, y_ref, carry_scr, h_scr) = refs
    else:
        (p_ref, mu_ref, w0_ref, w2_ref, a0_ref, a2_ref, g2_ref, kk_ref, ka_ref, rk_ref,
         lnw_ref, lnb_ref, y_ref, vf_out_ref, carry_scr, h_scr) = refs

    @pl.when(pl.program_id(1) == 0)
    def _():
        carry_scr[...] = jnp.zeros_like(carry_scr)
        h_scr[...] = jnp.zeros_like(h_scr)

    w = BRANCH_W
    rows4 = RW_HEADS * CHUNK
    r_i = _iota((rows4, w), 0)
    c_i = _iota((rows4, w), 1)
    bd = _div(r_i, CHUNK) == _div(c_i, RW_HEAD)
    eye = (r_i == c_i).astype(F32)
    t_row = _iota((CHUNK, w), 0)
    s_col = _mod(_iota((CHUNK, w), 1), CHUNK)
    strict = s_col < t_row
    incl = s_col <= t_row
    eye_c = (s_col == t_row).astype(F32)
    tri = (_iota((CHUNK, CHUNK), 0) >= _iota((CHUNK, CHUNK), 1)).astype(BF16)
    head_ones = bd.astype(BF16)
    row0 = _iota((CHUNK, RW_W), 0) == 0
    zeros_cw = jnp.zeros((CHUNK, w), F32)
    mu = mu_ref[...]
    inv_n = 1.0 / RW_HEAD

    def expand(x):
        return jnp.where(bd, _tile_rows4(x), 0.0)

    def headsum(x):
        return _mm(x, head_ones)

    def chunk(bi, c):
        rows = _chunk_rows(c)
        p = p_ref[bi, rows, :]
        pf = p[:, 0:RW_W]
        prev = jnp.where(row0, carry_scr[bi, 0:1, :], pltpu.roll(pf, 1, axis=0))
        carry_scr[bi, 0:1, :] = pf[CHUNK - 1:CHUNK, :]
        xs = pf + mu * (prev - pf)
        r = xs[:, 0:w]
        k = xs[:, w:2 * w]
        v = xs[:, 2 * w:3 * w]
        lr = xs[:, 3 * w:]
        wlow = _mm3(jnp.tanh(lr), w2_ref[...])
        alow = _mm(lr, a2_ref[...])
        g = _mm(_sigmoid(lr), g2_ref[...])
        if has_vgate:
            vlow = _mm(p[:, RW_W:], v2_ref[...])
        kk = k * kk_ref[...]
        kss = headsum(kk * kk)
        yield
        w_log = -_softplus(-(w0_ref[...] + wlow)) - 0.5
        lw = -jnp.exp(w_log)
        cum = _mm_exact_lhs(tri, lw)
        ag = _sigmoid(a0_ref[...] + alow)
        if has_vgate:
            v = v + (vf_ref[bi, rows, :] - v) * _sigmoid(v0_ref[...] + vlow)
        else:
            vf_out_ref[bi, rows, :] = v
        kk = kk * lax.rsqrt(jnp.maximum(kss, 1e-24))
        k2 = k * (1.0 + (ag - 1.0) * ka_ref[...])
        bvec = kk * ag
        rk_sum = headsum(r * k2 * rk_ref[...])
        v_e = expand(v)
        yield
        tot = cum[CHUNK - 1:CHUNK, :]
        e_neg = jnp.exp(-cum)
        e_tail = jnp.exp(tot - cum)
        a_n = -kk * jnp.exp(cum - lw)
        r_n = r * jnp.exp(cum)
        lhs = jnp.concatenate([a_n, r_n], axis=0)
        rhs = jnp.concatenate([expand(bvec * e_neg), expand(k2 * e_neg)], axis=0)
        aa = _mm_nt(lhs, rhs)
        yield
        a_ab = jnp.where(strict, aa[0:CHUNK, 0:w], 0.0)
        a_ak = jnp.where(strict, aa[0:CHUNK, w:], 0.0)
        a_rb = jnp.where(incl, aa[CHUNK:, 0:w], 0.0)
        a_rk = jnp.where(incl, aa[CHUNK:, w:], 0.0)
        tinv = eye_c + a_ab
        pw = a_ab
        pw_e = expand(pw)
        akv = _mm(a_ak, v_e)
        rkv = _mm(a_rk, v_e)
        for _ in range(5):
            pw = _mm(pw, pw_e)
            yield
            pw_e = expand(pw)
            tinv = tinv + _mm(tinv, pw_e)
        yield
        x_n = _mm(tinv, jnp.concatenate([expand(a_n), expand(akv)], axis=1))
        yield
        yq = _mm(a_rb, jnp.concatenate([expand(x_n[:, 0:w]), expand(x_n[:, w:])], axis=1))
        mn = _mm_tn(jnp.concatenate([bvec * e_tail, k2 * e_tail], axis=0),
                    jnp.concatenate([x_n, jnp.concatenate([zeros_cw, v], axis=1)], axis=0))
        yield
        q_n = r_n + yq[:, 0:w]
        ol_n = yq[:, w:] + rkv
        m_mat = jnp.where(bd, mn[:, 0:w], 0.0) + eye * jnp.exp(tot)
        n_mat = jnp.where(bd, mn[:, w:], 0.0)
        h = h_scr[bi]
        o = ol_n + _mm(q_n, h)
        h_scr[bi] = _mm(m_mat, h) + n_mat
        yield
        mean = headsum(o) * inv_n
        yield
        dlt = o - mean
        var = headsum(dlt * dlt) * inv_n
        yield
        o = dlt * lax.rsqrt(var + RW_LN_EPS) * lnw_ref[...] + lnb_ref[...]
        y_ref[bi, rows, :] = (o + rk_sum * v) * g

    _loop_streams(cpt, nrows, chunk, STREAM_DEPTH)


def _rwkv(pr, v_first, prm, tile):
    b, lp, pw = pr.shape
    has_vgate = v_first is not None
    cpt = tile // CHUNK
    vec = _const_spec((1, BRANCH_W))
    mat = _const_spec((LANES, BRANCH_W))
    nrows = _stream_rows(b)
    ins = [pr]
    specs = [_row_spec(tile, pw, nrows)]
    if has_vgate:
        ins.append(v_first)
        specs.append(_row_spec(tile, BRANCH_W, nrows))
    ins += [prm["mu"], prm["w0"], prm["w2"], prm["a0"], prm["a2"], prm["g2"], prm["k_k"], prm["k_a"],
            prm["r_k"], prm["ln_w"], prm["ln_b"]]
    specs += [_const_spec((1, RW_W)), vec, mat, vec, mat, mat, vec, vec, vec, vec, vec]
    out_shape = [jax.ShapeDtypeStruct((b, lp, BRANCH_W), F32)]
    out_specs = [_row_spec(tile, BRANCH_W, nrows)]
    if has_vgate:
        ins += [prm["v0"], prm["v2"]]
        specs += [vec, mat]
    else:
        out_shape.append(jax.ShapeDtypeStruct((b, lp, BRANCH_W), F32))
        out_specs.append(_row_spec(tile, BRANCH_W, nrows))
    outs = pl.pallas_call(
        functools.partial(_rw_kernel, cpt=cpt, nrows=nrows, has_vgate=has_vgate),
        grid=(b // nrows, lp // tile),
        in_specs=specs,
        out_specs=out_specs,
        out_shape=out_shape,
        scratch_shapes=[pltpu.VMEM((nrows, SUBLANES, RW_W), F32),
                        pltpu.VMEM((nrows, RW_HEADS * RW_HEAD, BRANCH_W), F32)],
        compiler_params=_params(("parallel", "arbitrary")),
        name="rwkv7",
    )(*ins)
    if has_vgate:
        return outs[0], v_first
    return outs[0], outs[1]


def _lru_kernel(p_ref, cw_ref, cb_ref, wg_ref, bg_ref, lam_ref, y_ref, x_scr, h_scr, a_scr, b_scr, *, tl):
    j = pl.program_id(1)

    @pl.when(j == 0)
    def _():
        x_scr[...] = jnp.zeros_like(x_scr)
        h_scr[...] = jnp.zeros_like(h_scr)

    w = BRANCH_W
    p = p_ref[0]
    xr = p[:, 0:w]
    gate = p[:, w:]
    carry8 = x_scr[...]
    xc = cb_ref[...] + cw_ref[LRU_CONV_W - 1:LRU_CONV_W, :] * xr
    for s in range(1, LRU_CONV_W):
        xc = xc + cw_ref[LRU_CONV_W - 1 - s:LRU_CONV_W - s, :] * _shift_rows(xr, s, carry8)
    x_scr[...] = xr[tl - SUBLANES:tl, :]
    gates = _mm(xc, wg_ref[...]) + bg_ref[...]
    rg = _sigmoid(gates[:, 0:w])
    ig = _sigmoid(gates[:, w:])
    log_a = -LRU_C * rg * _softplus(-lam_ref[...])
    a = jnp.exp(log_a)
    mult = jnp.sqrt(1.0 - jnp.exp(2.0 * log_a))
    ridx = j * tl + _iota((tl, w), 0)
    mult = jnp.where(ridx == PAD, 1.0, mult)
    bterm = jnp.where(ridx >= PAD, mult * ig * xc, 0.0)

    ng = tl // SUBLANES
    g_i = _iota((ng, LANES), 0)
    for hf in range(w // LANES):
        lanes = slice(hf * LANES, (hf + 1) * LANES)
        a_scr[hf] = a[:, lanes]
        b_scr[hf] = bterm[:, lanes]
        prods, sums = [], []
        for r in range(SUBLANES):
            a_r = a_scr[hf, pl.ds(r, ng, stride=SUBLANES), :]
            b_r = b_scr[hf, pl.ds(r, ng, stride=SUBLANES), :]
            prods.append(a_r if r == 0 else a_r * prods[-1])
            sums.append(b_r if r == 0 else a_r * sums[-1] + b_r)
        pt, ct = prods[-1], sums[-1]
        d = 1
        while d < ng:
            keep = g_i >= d
            ct = ct + pt * jnp.where(keep, pltpu.roll(ct, d, axis=0), 0.0)
            pt = pt * jnp.where(keep, pltpu.roll(pt, d, axis=0), 1.0)
            d *= 2
        h_end = ct + pt * h_scr[0:1, lanes]
        h_in = jnp.where(g_i == 0, h_scr[0:1, lanes], pltpu.roll(h_end, 1, axis=0))
        h_scr[0:1, lanes] = h_end[ng - 1:ng, :]
        for r in range(SUBLANES):
            b_scr[hf, pl.ds(r, ng, stride=SUBLANES), :] = sums[r] + prods[r] * h_in
        y_ref[0, :, lanes] = b_scr[hf] * _gelu(gate[:, lanes])


def _lru(pl_, cw, cb, wg, bg, lam, tile):
    b, lp, _ = pl_.shape
    vec = _const_spec((1, BRANCH_W))
    return pl.pallas_call(
        functools.partial(_lru_kernel, tl=tile),
        grid=(b, lp // tile),
        in_specs=[_row_spec(tile, LRU_W), _const_spec(cw.shape), vec, _const_spec(wg.shape),
                  _const_spec(bg.shape), vec],
        out_specs=_row_spec(tile, BRANCH_W),
        out_shape=jax.ShapeDtypeStruct((b, lp, BRANCH_W), F32),
        scratch_shapes=[pltpu.VMEM((SUBLANES, BRANCH_W), F32), pltpu.VMEM((SUBLANES, BRANCH_W), F32),
                        pltpu.VMEM((BRANCH_W // LANES, tile, LANES), F32),
                        pltpu.VMEM((BRANCH_W // LANES, tile, LANES), F32)],
        compiler_params=_params(("parallel", "arbitrary")),
        name="rglru",
    )(pl_, cw, cb, wg, bg, lam)


def _merge_kernel(h_ref, g_ref, u5_ref, y5_ref, yg_ref, yr_ref, yl_ref, d5_ref, wglu_ref, bglu_ref,
                  wgate_ref, bgate_ref, wbr_ref, wout_ref, o_ref, *, tm):
    x = h_ref[0]
    xn = _rmsnorm_rows(x, g_ref[...]).astype(BF16)
    ys = _gelu(y5_ref[0] + d5_ref[...] * u5_ref[0])
    ys = ys * _sigmoid(_mm(ys, wglu_ref[...]) + bglu_ref[...])
    branches = (ys, yg_ref[0], yr_ref[0], yl_ref[0])
    merged = None
    for i, yb in enumerate(branches):
        gate = _sigmoid(jnp.dot(xn, wgate_ref[i], preferred_element_type=F32) + bgate_ref[i])
        term = gate * _mm(yb, wbr_ref[i])
        merged = term if merged is None else merged + term
    upd = _mm(merged, wout_ref[...])
    ridx = pl.program_id(1) * tm + _iota((tm, 1), 0)
    o_ref[0] = x + jnp.where(ridx >= PAD, upd, 0.0)


def _merge(h, g, u5, y5, yg, yr, yl, d5, wglu, bglu, wgate, bgate, wbr, wout, tm):
    b, lp, d = h.shape
    ybs = _row_spec(tm, BRANCH_W)
    return pl.pallas_call(
        functools.partial(_merge_kernel, tm=tm),
        grid=(b, lp // tm),
        in_specs=[_row_spec(tm, d), _const_spec((1, d)), ybs, ybs, ybs, ybs, ybs,
                  _const_spec((1, BRANCH_W)), _const_spec(wglu.shape), _const_spec((1, BRANCH_W)),
                  _const_spec(wgate.shape), _const_spec(bgate.shape), _const_spec(wbr.shape),
                  _const_spec(wout.shape)],
        out_specs=_row_spec(tm, d),
        out_shape=jax.ShapeDtypeStruct((b, lp, d), F32),
        compiler_params=_params(("parallel", "parallel")),
        name="merge",
    )(h, g, u5, y5, yg, yr, yl, d5, wglu, bglu, wgate, bgate, wbr, wout)


FFN_COLS = 256


def _ffn_kernel(*refs, tm, final):
    if final:
        h_ref, g_ref, wup_ref, cw_ref, cb_ref, wdn_ref, gf_ref, o_ref, g_scr, act_scr = refs
    else:
        h_ref, g_ref, wup_ref, cw_ref, cb_ref, wdn_ref, o_ref, g_scr, act_scr = refs
    j = pl.program_id(1)

    @pl.when(j == 0)
    def _():
        g_scr[...] = jnp.zeros_like(g_scr)

    x = h_ref[0]
    xn = _rmsnorm_rows(x, g_ref[...]).astype(BF16)
    for c in range(D_FF // FFN_COLS):
        lo = c * FFN_COLS
        gate = jnp.dot(xn, wup_ref[:, lo:lo + FFN_COLS], preferred_element_type=F32)
        up = jnp.dot(xn, wup_ref[:, D_FF + lo:D_FF + lo + FFN_COLS], preferred_element_type=F32)
        carry8 = g_scr[:, lo:lo + FFN_COLS]
        g_scr[:, lo:lo + FFN_COLS] = gate[tm - SUBLANES:tm, :]
        gc = cb_ref[:, lo:lo + FFN_COLS] + cw_ref[FFN_CONV_W - 1:FFN_CONV_W, lo:lo + FFN_COLS] * gate
        for s in range(1, FFN_CONV_W):
            gc = gc + (cw_ref[FFN_CONV_W - 1 - s:FFN_CONV_W - s, lo:lo + FFN_COLS]
                       * _shift_rows(gate, s, carry8))
        act_scr[:, lo:lo + FFN_COLS] = (_silu(gc) * up).astype(BF16)
    acc = jnp.dot(act_scr[...], wdn_ref[...], preferred_element_type=F32)
    ridx = j * tm + _iota((tm, 1), 0)
    out = x + jnp.where(ridx >= PAD, acc, 0.0)
    if final:
        out = _rmsnorm_rows(out, gf_ref[...])
    o_ref[0] = out


def _ffn(h, g, wup, cw, cb, wdn, gf, tm):
    b, lp, d = h.shape
    final = gf is not None
    ins = [h, g, wup, cw, cb, wdn]
    specs = [_row_spec(tm, d), _const_spec((1, d)), _const_spec(wup.shape), _const_spec(cw.shape),
             _const_spec(cb.shape), _const_spec(wdn.shape)]
    if final:
        ins.append(gf)
        specs.append(_const_spec((1, d)))
    return pl.pallas_call(
        functools.partial(_ffn_kernel, tm=tm, final=final),
        grid=(b, lp // tm),
        in_specs=specs,
        out_specs=_row_spec(tm, d),
        out_shape=jax.ShapeDtypeStruct((b, lp, d), F32),
        scratch_shapes=[pltpu.VMEM((SUBLANES, D_FF), F32), pltpu.VMEM((tm, D_FF), BF16)],
        compiler_params=_params(("parallel", "arbitrary")),
        name="ffn",
    )(*ins)


def _pad_rows(m, rows, at=0):
    return jnp.zeros((rows, m.shape[1]), m.dtype).at[at:at + m.shape[0]].set(m)


def _block_diag(w):
    hn, n, _ = w.shape
    out = jnp.zeros((hn * n, hn * n), w.dtype)
    for i in range(hn):
        out = out.at[i * n:(i + 1) * n, i * n:(i + 1) * n].set(w[i])
    return out


def kernel(x, meta, w_in, rw_mu, s5_a_re, s5_a_im, s5_log_dt, s5_b_re, s5_b_im, s5_c_re, s5_c_im, s5_d, s5_w_glu, s5_b_glu, gla_w_gate, gla_b_gate, gla_norm, rw_w0, rw_w2, rw_a0, rw_a2, rw_g2, rw_k_k, rw_k_a, rw_r_k, rw_ln_w, rw_ln_b, rw_v0, rw_v1, rw_v2, lru_conv_w, lru_conv_b, lru_w_a, lru_b_a, lru_w_i, lru_b_i, lru_lam, w_branch, w_merge_gate, b_merge_gate, w_out, norm_mix, norm_ffn, ffn_w_up, ffn_conv_w, ffn_conv_b, ffn_w_down, norm_final):
    bsz, seq, d = x.shape
    assert d == D_MODEL and seq % CHUNK == 0
    lp = FRONT + seq
    tm = _pick_tile(lp, 1040, SUBLANES)
    tchunk = _pick_tile(lp, 832)
    row = lambda v: v.astype(F32).reshape(1, -1)

    h = jnp.concatenate([jnp.zeros((bsz, PAD, d), x.dtype),
                         jnp.broadcast_to(meta.astype(x.dtype)[None], (bsz, N_META, d)), x], axis=1)
    v_first = None
    for l in range(DEPTH):
        wl = w_in[l]
        o_gla = BRANCH_W
        o_rw = o_gla + 784
        o_lru = o_rw + 896
        hq = GLA_HEADS * GLA_DK
        gla_cols = jnp.concatenate([
            wl[:, o_gla:o_gla + 2 * hq + BRANCH_W],
            wl[:, o_gla + 2 * hq + BRANCH_W + GLA_GATE_RANK:o_rw],
            wl[:, o_gla + 2 * hq + BRANCH_W:o_gla + 2 * hq + BRANCH_W + GLA_GATE_RANK],
            jnp.zeros((d, LANES - GLA_GATE_RANK), wl.dtype)], axis=1)
        rw_cols = wl[:, o_rw:o_lru]
        if l > 0:
            rw_cols = jnp.concatenate([rw_cols, rw_v1[l - 1],
                                       jnp.zeros((d, LANES - RW_V_RANK), wl.dtype)], axis=1)
        wcat = jnp.concatenate([wl[:, 0:BRANCH_W], gla_cols, rw_cols, wl[:, o_lru:]], axis=1).astype(BF16)
        widths = (BRANCH_W, GLA_W, RW_WV if l > 0 else RW_W, LRU_W)
        p5, pg, pr, pu = _proj(h, row(norm_mix[l]), wcat, widths, tm)

        y5 = _s5(p5, _s5_mats(s5_a_re[l], s5_a_im[l], s5_log_dt[l], s5_b_re[l], s5_b_im[l],
                              s5_c_re[l], s5_c_im[l]))
        yg = _gla(pg, _pad_rows(gla_w_gate[l].astype(F32), LANES), row(gla_b_gate[l]), row(gla_norm[l]), tchunk)
        prm = dict(mu=row(rw_mu[l]), w0=row(rw_w0[l]), w2=_pad_rows(rw_w2[l].astype(F32), LANES, 0),
                   a0=row(rw_a0[l]), a2=_pad_rows(rw_a2[l].astype(F32), LANES, RW_W_RANK),
                   g2=_pad_rows(rw_g2[l].astype(F32), LANES, RW_W_RANK + RW_A_RANK),
                   k_k=row(rw_k_k[l]), k_a=row(rw_k_a[l]), r_k=row(rw_r_k[l]),
                   ln_w=row(rw_ln_w[l]), ln_b=row(rw_ln_b[l]))
        if l > 0:
            prm["v0"] = row(rw_v0[l - 1])
            prm["v2"] = _pad_rows(rw_v2[l - 1].astype(F32), LANES)
        yr, v_first = _rwkv(pr, v_first, prm, tchunk)
        wgi = jnp.concatenate([_block_diag(lru_w_a[l]), _block_diag(lru_w_i[l])], axis=1).astype(BF16)
        bgi = jnp.concatenate([lru_b_a[l], lru_b_i[l]]).astype(F32).reshape(1, -1)
        yl = _lru(pu, lru_conv_w[l].astype(F32), row(lru_conv_b[l]), wgi, bgi, row(lru_lam[l]), tchunk)

        h = _merge(h, row(norm_mix[l]), p5, y5, yg, yr, yl, row(s5_d[l]), s5_w_glu[l].astype(BF16),
                   row(s5_b_glu[l]), w_merge_gate[l].astype(BF16),
                   b_merge_gate[l].astype(F32).reshape(4, 1, d), w_branch[l].astype(BF16),
                   w_out[l].astype(BF16), tm)
        h = _ffn(h, row(norm_ffn[l]), ffn_w_up[l].astype(BF16), ffn_conv_w[l].astype(F32),
                 row(ffn_conv_b[l]), ffn_w_down[l].astype(BF16),
                 row(norm_final) if l == DEPTH - 1 else None, tm)
    return h[:, FRONT:]
```

```python
import functools
import math

import numpy as np
import jax
import jax.numpy as jnp
from jax import lax
from jax.experimental import pallas as pl
from jax.experimental.pallas import tpu as pltpu

F32 = jnp.float32
BF16 = jnp.bfloat16

D_MODEL = 1024
DEPTH = 4
N_META = 16
BRANCH_W = 256
S5_GROUP = 16
S5_GROUPS = 16
S5_STATE = 64
S5_BLOCK = 16
GLA_HEADS = 4
GLA_DK = 32
GLA_DV = 64
GLA_GATE_RANK = 16
GLA_TAU = 16.0
CHUNK = 64
RW_HEADS = 4
RW_HEAD = 64
RW_W_RANK = 32
RW_A_RANK = 32
RW_V_RANK = 16
RW_G_RANK = 64
RW_LN_EPS = 64e-5
LRU_C = 8.0
LRU_CONV_W = 4
D_FF = 2816
FFN_CONV_W = 3
RMS_EPS = 1e-6

PAD = (-N_META) % CHUNK
FRONT = PAD + N_META
LANES = 128
SUBLANES = 8
VMEM_LIMIT = 56 * 1024 * 1024

GLA_W = 896
RW_W = 896
RW_WV = 1024
LRU_W = 512


def _mm(a, b):
    return jnp.dot(a.astype(BF16), b.astype(BF16), preferred_element_type=F32)


def _mm_nt(a, b):
    return lax.dot_general(a.astype(BF16), b.astype(BF16), (((1,), (1,)), ((), ())),
                           preferred_element_type=F32)


def _mm_tn(a, b):
    return lax.dot_general(a.astype(BF16), b.astype(BF16), (((0,), (0,)), ((), ())),
                           preferred_element_type=F32)


def _split2(a):
    hi = a.astype(BF16)
    lo = (a - hi.astype(F32)).astype(BF16)
    return hi, lo


def _split3(a):
    hi = a.astype(BF16)
    r1 = a - hi.astype(F32)
    mid = r1.astype(BF16)
    lo = (r1 - mid.astype(F32)).astype(BF16)
    return hi, mid, lo


def _dg(a, b, dims):
    return lax.dot_general(a, b, (dims, ((), ())), preferred_element_type=F32)


def _mm3_dims(a, b, dims):
    ah, al = _split2(a)
    bh, bl = _split2(b)
    return _dg(ah, bh, dims) + (_dg(ah, bl, dims) + _dg(al, bh, dims))


_NN = ((1,), (0,))
_NT = ((1,), (1,))
_TN = ((0,), (0,))


def _mm3(a, b):
    return _mm3_dims(a, b, _NN)


def _mm3_nt(a, b):
    return _mm3_dims(a, b, _NT)


def _mm3_tn(a, b):
    return _mm3_dims(a, b, _TN)


def _mm_exact_rhs(a, b_bf16):
    hi, mid, lo = _split3(a)
    return _dg(hi, b_bf16, _NN) + (_dg(mid, b_bf16, _NN) + _dg(lo, b_bf16, _NN))


def _mm_exact_lhs(a_bf16, b):
    hi, mid, lo = _split3(b)
    return _dg(a_bf16, hi, _NN) + (_dg(a_bf16, mid, _NN) + _dg(a_bf16, lo, _NN))


def _softplus(x):
    return jnp.maximum(x, 0.0) + jnp.log1p(jnp.exp(-jnp.abs(x)))


def _sigmoid(x):
    return jax.nn.sigmoid(x)


def _silu(x):
    return x * jax.nn.sigmoid(x)


def _gelu(x):
    return jax.nn.gelu(x)


def _rmsnorm_rows(x, g):
    ms = jnp.mean(x * x, axis=-1, keepdims=True)
    return x * lax.rsqrt(ms + RMS_EPS) * g


def _iota(shape, dim):
    return lax.broadcasted_iota(jnp.int32, shape, dim)


def _div(x, n):
    return x >> int(math.log2(n))


def _mod(x, n):
    return x & (n - 1)


def _expand_heads(x, block_mask):
    return jnp.where(block_mask, _tile_rows4(x), 0.0)


def _tile_rows4(x):
    return jnp.concatenate([x, x, x, x], axis=0)


def _fold_rows4(z):
    n = z.shape[0] // 4
    return (z[0:n] + z[n:2 * n]) + (z[2 * n:3 * n] + z[3 * n:4 * n])


def _shift_rows(x, s, carry8):
    xs = pltpu.roll(x, s, axis=0)
    top = jnp.where(_iota((SUBLANES, x.shape[1]), 0) < s,
                    pltpu.roll(carry8, s, axis=0), xs[0:SUBLANES])
    return jnp.concatenate([top, xs[SUBLANES:]], axis=0)


def _lockstep(gens):
    gens = list(gens)
    while gens:
        alive = []
        for g in gens:
            try:
                next(g)
                alive.append(g)
            except StopIteration:
                pass
        gens = alive


def _loop_streams(n, rows, chunk_gen, depth):
    def trip(i, carry):
        _lockstep([chunk_gen(r, depth * i + d) for d in range(depth) for r in range(rows)])
        return carry

    if n >= depth:
        lax.fori_loop(0, n // depth, trip, 0)
    tail = n % depth
    if tail:
        _lockstep([chunk_gen(r, n - tail + d) for d in range(tail) for r in range(rows)])


def _chunk_rows(c):
    start = c * CHUNK
    return pl.ds(start if isinstance(c, int) else pl.multiple_of(start, CHUNK), CHUNK)


def _const_spec(shape):
    nd = len(shape)
    return pl.BlockSpec(shape, lambda *_: (0,) * nd, pipeline_mode=pl.Buffered(1))


def _row_spec(rows, width, nb=1):
    return pl.BlockSpec((nb, rows, width), lambda b, j: (b, j, 0))


STREAM_ROWS = 2
STREAM_DEPTH = 5


def _stream_rows(b):
    return STREAM_ROWS if b % STREAM_ROWS == 0 else 1


def _params(sem):
    return pltpu.CompilerParams(dimension_semantics=sem, vmem_limit_bytes=VMEM_LIMIT)


def _pick_tile(lp, cap, step=CHUNK):
    best = step
    t = step
    while t <= min(lp, cap):
        if lp % t == 0:
            best = t
        t += step
    return best


def _proj_kernel(h_ref, g_ref, w_ref, *out_refs, widths):
    xn = _rmsnorm_rows(h_ref[0], g_ref[...]).astype(BF16)
    off = 0
    for o_ref, w in zip(out_refs, widths):
        o_ref[0] = jnp.dot(xn, w_ref[:, off:off + w], preferred_element_type=F32)
        off += w


def _proj(h, g, w, widths, tm):
    b, lp, d = h.shape
    nt = lp // tm
    return pl.pallas_call(
        functools.partial(_proj_kernel, widths=widths),
        grid=(b, nt),
        in_specs=[_row_spec(tm, d), _const_spec((1, d)), _const_spec(w.shape)],
        out_specs=[_row_spec(tm, wd) for wd in widths],
        out_shape=[jax.ShapeDtypeStruct((b, lp, wd), F32) for wd in widths],
        compiler_params=_params(("parallel", "parallel")),
        name="proj",
    )(h, g, w)


def _cmul(pr, pi, zr, zi):
    return pr * zr - pi * zi, pr * zi + pi * zr


def _s5_kernel(u_ref, kd_ref, bb_ref, cc_ref, pe_ref, pc_ref, a_ref, y_ref,
               io_scr, xs_scr, ys_scr, e_scr, sp_scr, *, nb, nbp):
    ns = S5_GROUPS * S5_STATE
    pad = nbp - nb
    halves = BRANCH_W // LANES
    for hf in range(halves):
        io_scr[hf] = u_ref[0, :, hf * LANES:(hf + 1) * LANES]
    for s in range(S5_BLOCK):
        for hf in range(halves):
            xs_scr[s, 0:nb, hf * LANES:(hf + 1) * LANES] = (
                io_scr[hf, pl.ds(s, nb, stride=S5_BLOCK), :].astype(BF16))
        if pad:
            xs_scr[s, nb:nbp, :] = jnp.zeros((pad, BRANCH_W), BF16)
    if pad:
        sp_scr[nb:nbp, :] = jnp.zeros((pad, 2 * ns), F32)

    e_re = None
    for s in range(S5_BLOCK):
        z = jnp.dot(xs_scr[s], bb_ref[...], preferred_element_type=F32)
        zr, zi = _cmul(pe_ref[s:s + 1, 0:ns], pe_ref[s:s + 1, ns:], z[:, 0:ns], z[:, ns:])
        e_re, e_im = (zr, zi) if e_re is None else (e_re + zr, e_im + zi)
    e_scr[:, 0:ns] = e_re
    e_scr[:, ns:] = e_im

    ar = a_ref[0:1, 0:ns]
    ai = a_ref[0:1, ns:]

    def step(k, carry):
        sr, si = carry
        sp_scr[pl.ds(k, 1), 0:ns] = sr
        sp_scr[pl.ds(k, 1), ns:] = si
        nr, ni = _cmul(ar, ai, sr, si)
        return nr + e_scr[pl.ds(k, 1), 0:ns], ni + e_scr[pl.ds(k, 1), ns:]

    zero = jnp.zeros((1, ns), F32)
    lax.fori_loop(0, nb, step, (zero, zero))

    for d in range(S5_BLOCK):
        m = S5_BLOCK - d
        prod = jnp.dot(xs_scr[0:m].reshape(m * nbp, BRANCH_W), kd_ref[d], preferred_element_type=F32)
        prod = prod.reshape(m, nbp, BRANCH_W)
        if d == 0:
            ys_scr[...] = prod
        else:
            ys_scr[d:S5_BLOCK] = ys_scr[d:S5_BLOCK] + prod

    spr = sp_scr[:, 0:ns]
    spi = sp_scr[:, ns:]
    for t in range(S5_BLOCK):
        gr, gi = _cmul(pc_ref[t:t + 1, 0:ns], pc_ref[t:t + 1, ns:], spr, spi)
        g = jnp.concatenate([gr, gi], axis=1).astype(BF16)
        yt = ys_scr[t] + jnp.dot(g, cc_ref[...], preferred_element_type=F32)
        for hf in range(halves):
            io_scr[hf, pl.ds(t, nb, stride=S5_BLOCK), :] = yt[0:nb, hf * LANES:(hf + 1) * LANES]
    for hf in range(halves):
        y_ref[0, :, hf * LANES:(hf + 1) * LANES] = io_scr[hf]


def _s5_mats(a_re, a_im, log_dt, b_re, b_im, c_re, c_im):
    lr, li = a_re.astype(F32), a_im.astype(F32)
    dt = jnp.exp(log_dt.astype(F32))[:, None]
    mag = jnp.exp(lr * dt)
    abr, abi = mag * jnp.cos(li * dt), mag * jnp.sin(li * dt)
    den = lr * lr + li * li
    cr = ((abr - 1.0) * lr + abi * li) / den
    ci = (abi * lr - (abr - 1.0) * li) / den
    d = jnp.arange(S5_BLOCK + 1, dtype=F32)[:, None, None]
    pmag = jnp.exp(lr * dt * d)
    p_re, p_im = pmag * jnp.cos(li * dt * d), pmag * jnp.sin(li * dt * d)
    bb_re = cr[..., None] * b_re - ci[..., None] * b_im
    bb_im = cr[..., None] * b_im + ci[..., None] * b_re
    cp_re = c_re[None] * p_re[:, :, None, :] - c_im[None] * p_im[:, :, None, :]
    cp_im = c_re[None] * p_im[:, :, None, :] + c_im[None] * p_re[:, :, None, :]
    hp = lax.Precision.HIGHEST
    kd = (jnp.einsum("dgon,gni->dgoi", cp_re, bb_re, precision=hp)
          - jnp.einsum("dgon,gni->dgoi", cp_im, bb_im, precision=hp))
    eye_g = jnp.eye(S5_GROUPS, dtype=F32)
    kmat = jnp.einsum("dgoi,gq->dgiqo", kd[:S5_BLOCK], eye_g).reshape(S5_BLOCK, BRANCH_W, BRANCH_W)
    ns = S5_GROUPS * S5_STATE
    bmat = jnp.concatenate([jnp.einsum("gni,gq->giqn", bb_re, eye_g).reshape(BRANCH_W, ns),
                            jnp.einsum("gni,gq->giqn", bb_im, eye_g).reshape(BRANCH_W, ns)], axis=1)
    cmat = jnp.concatenate([jnp.einsum("gon,gq->gnqo", c_re.astype(F32), eye_g).reshape(ns, BRANCH_W),
                            jnp.einsum("gon,gq->gnqo", -c_im.astype(F32), eye_g).reshape(ns, BRANCH_W)], axis=0)
    flat = lambda z: z.reshape(z.shape[0], ns)
    t = np.arange(S5_BLOCK)
    pe = jnp.concatenate([flat(p_re[S5_BLOCK - 1 - t]), flat(p_im[S5_BLOCK - 1 - t])], axis=1)
    pc = jnp.concatenate([flat(p_re[1:]), flat(p_im[1:])], axis=1)
    a16 = jnp.concatenate([flat(p_re[S5_BLOCK:]), flat(p_im[S5_BLOCK:])], axis=1)
    a16 = jnp.pad(a16, ((0, SUBLANES - 1), (0, 0)))
    return kmat.astype(BF16), bmat.astype(BF16), cmat.astype(BF16), pe, pc, a16


def _s5(p5, mats):
    kmat, bmat, cmat, pe, pc, a16 = mats
    b, lp, _ = p5.shape
    nb = lp // S5_BLOCK
    nbp = -(-nb // 16) * 16
    ns2 = 2 * S5_GROUPS * S5_STATE
    return pl.pallas_call(
        functools.partial(_s5_kernel, nb=nb, nbp=nbp),
        grid=(b,),
        in_specs=[pl.BlockSpec((1, lp, BRANCH_W), lambda i: (i, 0, 0)), _const_spec(kmat.shape),
                  _const_spec(bmat.shape), _const_spec(cmat.shape), _const_spec(pe.shape),
                  _const_spec(pc.shape), _const_spec(a16.shape)],
        out_specs=pl.BlockSpec((1, lp, BRANCH_W), lambda i: (i, 0, 0)),
        out_shape=jax.ShapeDtypeStruct((b, lp, BRANCH_W), F32),
        scratch_shapes=[pltpu.VMEM((BRANCH_W // LANES, lp, LANES), F32),
                        pltpu.VMEM((S5_BLOCK, nbp, BRANCH_W), BF16),
                        pltpu.VMEM((S5_BLOCK, nbp, BRANCH_W), F32),
                        pltpu.VMEM((nbp, ns2), F32),
                        pltpu.VMEM((nbp, ns2), F32)],
        compiler_params=_params(("parallel",)),
        name="s5",
    )(p5, kmat, bmat, cmat, pe, pc, a16)


def _gla_kernel(p_ref, wg_ref, bg_ref, nw_ref, y_ref, st_scr, *, cpt, nrows):
    @pl.when(pl.program_id(1) == 0)
    def _():
        st_scr[...] = jnp.zeros_like(st_scr)

    hq = GLA_HEADS * GLA_DK
    hv = GLA_HEADS * GLA_DV
    rows4 = GLA_HEADS * CHUNK
    tri = (_iota((CHUNK, CHUNK), 0) >= _iota((CHUNK, CHUNK), 1)).astype(BF16)
    bd_k = _div(_iota((rows4, hq), 0), CHUNK) == _div(_iota((rows4, hq), 1), GLA_DK)
    causal = _mod(_iota((CHUNK, rows4), 1), CHUNK) <= _iota((CHUNK, rows4), 0)
    bd_v = _div(_iota((rows4, hv), 0), CHUNK) == _div(_iota((rows4, hv), 1), GLA_DV)
    bd_st = _div(_iota((hv, hq), 0), GLA_DV) == _div(_iota((hv, hq), 1), GLA_DK)
    head_ones = (_div(_iota((hv, hv), 0), GLA_DV) == _div(_iota((hv, hv), 1), GLA_DV)).astype(BF16)
    wg = wg_ref[...]
    bg = bg_ref[...]
    nw = nw_ref[...]

    def chunk(bi, c):
        rows = _chunk_rows(c)
        p = p_ref[bi, rows, :]
        q = p[:, 0:hq]
        k = p[:, hq:2 * hq]
        v = p[:, 2 * hq:2 * hq + hv]
        og = p[:, 2 * hq + hv:2 * hq + 2 * hv]
        gd = p[:, 2 * hq + 2 * hv:]
        x = _mm3(gd, wg) + bg
        yield
        g = (jnp.minimum(x, 0.0) - jnp.log1p(jnp.exp(-jnp.abs(x)))) / GLA_TAU
        gc = _mm_exact_lhs(tri, g)
        yield
        gl = gc[CHUNK - 1:CHUNK, :]
        qg = q * (GLA_DK ** -0.5) * jnp.exp(gc)
        kg = k * jnp.exp(-gc)
        kl = k * jnp.exp(gl - gc)
        kg_e = _expand_heads(kg, bd_k)
        att = _mm_nt(qg, kg_e)
        kv = _mm_tn(v, kl)
        yield
        att = jnp.where(causal, att, 0.0)
        v_e = _expand_heads(v, bd_v)
        st = st_scr[bi]
        o = _mm(att, v_e) + _mm_nt(qg, st)
        st_scr[bi] = st * jnp.exp(gl) + jnp.where(bd_st, kv, 0.0)
        yield
        ms = _mm(o * o, head_ones) * (1.0 / GLA_DV)
        yield
        o = o * lax.rsqrt(ms + RMS_EPS) * nw
        y_ref[bi, rows, :] = o * _silu(og)

    _loop_streams(cpt, nrows, chunk, STREAM_DEPTH)


def _gla(pg, w_gate_pad, b_gate, norm_w, tile):
    b, lp, _ = pg.shape
    cpt = tile // CHUNK
    nrows = _stream_rows(b)
    return pl.pallas_call(
        functools.partial(_gla_kernel, cpt=cpt, nrows=nrows),
        grid=(b // nrows, lp // tile),
        in_specs=[_row_spec(tile, GLA_W, nrows), _const_spec(w_gate_pad.shape),
                  _const_spec(b_gate.shape), _const_spec(norm_w.shape)],
        out_specs=_row_spec(tile, BRANCH_W, nrows),
        out_shape=jax.ShapeDtypeStruct((b, lp, BRANCH_W), F32),
        scratch_shapes=[pltpu.VMEM((nrows, GLA_HEADS * GLA_DV, GLA_HEADS * GLA_DK), F32)],
        compiler_params=_params(("parallel", "arbitrary")),
        name="gla",
    )(pg, w_gate_pad, b_gate, norm_w)


def _rw_kernel(*refs, cpt, nrows, has_vgate):
    if has_vgate:
        (p_ref, vf_ref, mu_ref, w0_ref, w2_ref, a0_ref, a2_ref, g2_ref, kk_ref, ka_ref, rk_ref,
         lnw_ref, lnb_ref, v0_ref, v2_ref, y_ref, carry_scr, h_scr) = refs
    else:
        (p_ref, mu_ref, w0_ref, w2_ref, a0_ref, a2_ref, g2_ref, kk_ref, ka_ref, rk_ref,
         lnw_ref, lnb_ref, y_ref, vf_out_ref, carry_scr, h_scr) = refs

    @pl.when(pl.program_id(1) == 0)
    def _():
        carry_scr[...] = jnp.zeros_like(carry_scr)
        h_scr[...] = jnp.zeros_like(h_scr)

    w = BRANCH_W
    rows4 = RW_HEADS * CHUNK
    r_i = _iota((rows4, w), 0)
    c_i = _iota((rows4, w), 1)
    bd = _div(r_i, CHUNK) == _div(c_i, RW_HEAD)
    eye = (r_i == c_i).astype(F32)
    t_row = _iota((CHUNK, w), 0)
    s_col = _mod(_iota((CHUNK, w), 1), CHUNK)
    strict = s_col < t_row
    incl = s_col <= t_row
    eye_c = (s_col == t_row).astype(F32)
    tri = (_iota((CHUNK, CHUNK), 0) >= _iota((CHUNK, CHUNK), 1)).astype(BF16)
    head_ones = bd.astype(BF16)
    row0 = _iota((CHUNK, RW_W), 0) == 0
    zeros_cw = jnp.zeros((CHUNK, w), F32)
    mu = mu_ref[...]
    inv_n = 1.0 / RW_HEAD

    def expand(x):
        return _expand_heads(x, bd)

    def headsum(x):
        return _mm(x, head_ones)

    def chunk(bi, c):
        rows = _chunk_rows(c)
        p = p_ref[bi, rows, :]
        pf = p[:, 0:RW_W]
        prev = jnp.where(row0, carry_scr[bi, 0:1, :], pltpu.roll(pf, 1, axis=0))
        carry_scr[bi, 0:1, :] = pf[CHUNK - 1:CHUNK, :]
        xs = pf + mu * (prev - pf)
        r = xs[:, 0:w]
        k = xs[:, w:2 * w]
        v = xs[:, 2 * w:3 * w]
        lr = xs[:, 3 * w:]
        wlow = _mm3(jnp.tanh(lr), w2_ref[...])
        alow = _mm(lr, a2_ref[...])
        g = _mm(_sigmoid(lr), g2_ref[...])
        if has_vgate:
            vlow = _mm(p[:, RW_W:], v2_ref[...])
        kk = k * kk_ref[...]
        kss = headsum(kk * kk)
        yield
        w_log = -_softplus(-(w0_ref[...] + wlow)) - 0.5
        lw = -jnp.exp(w_log)
        cum = _mm_exact_lhs(tri, lw)
        ag = _sigmoid(a0_ref[...] + alow)
        if has_vgate:
            v = v + (vf_ref[bi, rows, :] - v) * _sigmoid(v0_ref[...] + vlow)
        else:
            vf_out_ref[bi, rows, :] = v
        kk = kk * lax.rsqrt(jnp.maximum(kss, 1e-24))
        k2 = k * (1.0 + (ag - 1.0) * ka_ref[...])
        bvec = kk * ag
        rk_sum = headsum(r * k2 * rk_ref[...])
        v_e = expand(v)
        yield
        tot = cum[CHUNK - 1:CHUNK, :]
        e_neg = jnp.exp(-cum)
        e_tail = jnp.exp(tot - cum)
        a_n = -kk * jnp.exp(cum - lw)
        r_n = r * jnp.exp(cum)
        lhs = jnp.concatenate([a_n, r_n], axis=0)
        rhs = jnp.concatenate([expand(bvec * e_neg), expand(k2 * e_neg)], axis=0)
        aa = _mm_nt(lhs, rhs)
        yield
        a_ab = jnp.where(strict, aa[0:CHUNK, 0:w], 0.0)
        a_ak = jnp.where(strict, aa[0:CHUNK, w:], 0.0)
        a_rb = jnp.where(incl, aa[CHUNK:, 0:w], 0.0)
        a_rk = jnp.where(incl, aa[CHUNK:, w:], 0.0)
        tinv = eye_c + a_ab
        pw = a_ab
        pw_e = expand(pw)
        akv = _mm(a_ak, v_e)
        rkv = _mm(a_rk, v_e)
        for _ in range(5):
            pw = _mm(pw, pw_e)
            yield
            pw_e = expand(pw)
            tinv = tinv + _mm(tinv, pw_e)
        yield
        x_n = _mm(tinv, jnp.concatenate([expand(a_n), expand(akv)], axis=1))
        yield
        yq = _mm(a_rb, jnp.concatenate([expand(x_n[:, 0:w]), expand(x_n[:, w:])], axis=1))
        mn = _mm_tn(jnp.concatenate([bvec * e_tail, k2 * e_tail], axis=0),
                    jnp.concatenate([x_n, jnp.concatenate([zeros_cw, v], axis=1)], axis=0))
        yield
        q_n = r_n + yq[:, 0:w]
        ol_n = yq[:, w:] + rkv
        m_mat = jnp.where(bd, mn[:, 0:w], 0.0) + eye * jnp.exp(tot)
        n_mat = jnp.where(bd, mn[:, w:], 0.0)
        h = h_scr[bi]
        o = ol_n + _mm(q_n, h)
        h_scr[bi] = _mm(m_mat, h) + n_mat
        yield
        mean = headsum(o) * inv_n
        yield
        dlt = o - mean
        var = headsum(dlt * dlt) * inv_n
        yield
        o = dlt * lax.rsqrt(var + RW_LN_EPS) * lnw_ref[...] + lnb_ref[...]
        y_ref[bi, rows, :] = (o + rk_sum * v) * g

    _loop_streams(cpt, nrows, chunk, STREAM_DEPTH)


def _rwkv(pr, v_first, prm, tile):
    b, lp, pw = pr.shape
    has_vgate = v_first is not None
    cpt = tile // CHUNK
    vec = _const_spec((1, BRANCH_W))
    mat = _const_spec((LANES, BRANCH_W))
    nrows = _stream_rows(b)
    ins = [pr]
    specs = [_row_spec(tile, pw, nrows)]
    if has_vgate:
        ins.append(v_first)
        specs.append(_row_spec(tile, BRANCH_W, nrows))
    ins += [prm["mu"], prm["w0"], prm["w2"], prm["a0"], prm["a2"], prm["g2"], prm["k_k"], prm["k_a"],
            prm["r_k"], prm["ln_w"], prm["ln_b"]]
    specs += [_const_spec((1, RW_W)), vec, mat, vec, mat, mat, vec, vec, vec, vec, vec]
    out_shape = [jax.ShapeDtypeStruct((b, lp, BRANCH_W), F32)]
    out_specs = [_row_spec(tile, BRANCH_W, nrows)]
    if has_vgate:
        ins += [prm["v0"], prm["v2"]]
        specs += [vec, mat]
    else:
        out_shape.append(jax.ShapeDtypeStruct((b, lp, BRANCH_W), F32))
        out_specs.append(_row_spec(tile, BRANCH_W, nrows))
    outs = pl.pallas_call(
        functools.partial(_rw_kernel, cpt=cpt, nrows=nrows, has_vgate=has_vgate),
        grid=(b // nrows, lp // tile),
        in_specs=specs,
        out_specs=out_specs,
        out_shape=out_shape,
        scratch_shapes=[pltpu.VMEM((nrows, SUBLANES, RW_W), F32),
                        pltpu.VMEM((nrows, RW_HEADS * RW_HEAD, BRANCH_W), F32)],
        compiler_params=_params(("parallel", "arbitrary")),
        name="rwkv7",
    )(*ins)
    if has_vgate:
        return outs[0], v_first
    return outs[0], outs[1]


def _lru_kernel(p_ref, cw_ref, cb_ref, wg_ref, bg_ref, lam_ref, y_ref, x_scr, h_scr, a_scr, b_scr, *, tl):
    j = pl.program_id(1)

    @pl.when(j == 0)
    def _():
        x_scr[...] = jnp.zeros_like(x_scr)
        h_scr[...] = jnp.zeros_like(h_scr)

    w = BRANCH_W
    p = p_ref[0]
    xr = p[:, 0:w]
    gate = p[:, w:]
    carry8 = x_scr[...]
    xc = cb_ref[...] + cw_ref[LRU_CONV_W - 1:LRU_CONV_W, :] * xr
    for s in range(1, LRU_CONV_W):
        xc = xc + cw_ref[LRU_CONV_W - 1 - s:LRU_CONV_W - s, :] * _shift_rows(xr, s, carry8)
    x_scr[...] = xr[tl - SUBLANES:tl, :]
    gates = _mm(xc, wg_ref[...]) + bg_ref[...]
    rg = _sigmoid(gates[:, 0:w])
    ig = _sigmoid(gates[:, w:])
    log_a = -LRU_C * rg * _softplus(-lam_ref[...])
    a = jnp.exp(log_a)
    mult = jnp.sqrt(1.0 - jnp.exp(2.0 * log_a))
    ridx = j * tl + _iota((tl, w), 0)
    mult = jnp.where(ridx == PAD, 1.0, mult)
    bterm = jnp.where(ridx >= PAD, mult * ig * xc, 0.0)

    ng = tl // SUBLANES
    g_i = _iota((ng, LANES), 0)
    for hf in range(w // LANES):
        lanes = slice(hf * LANES, (hf + 1) * LANES)
        a_scr[hf] = a[:, lanes]
        b_scr[hf] = bterm[:, lanes]
        prods, sums = [], []
        for r in range(SUBLANES):
            a_r = a_scr[hf, pl.ds(r, ng, stride=SUBLANES), :]
            b_r = b_scr[hf, pl.ds(r, ng, stride=SUBLANES), :]
            prods.append(a_r if r == 0 else a_r * prods[-1])
            sums.append(b_r if r == 0 else a_r * sums[-1] + b_r)
        pt, ct = prods[-1], sums[-1]
        d = 1
        while d < ng:
            keep = g_i >= d
            ct = ct + pt * jnp.where(keep, pltpu.roll(ct, d, axis=0), 0.0)
            pt = pt * jnp.where(keep, pltpu.roll(pt, d, axis=0), 1.0)
            d *= 2
        h_end = ct + pt * h_scr[0:1, lanes]
        h_in = jnp.where(g_i == 0, h_scr[0:1, lanes], pltpu.roll(h_end, 1, axis=0))
        h_scr[0:1, lanes] = h_end[ng - 1:ng, :]
        for r in range(SUBLANES):
            b_scr[hf, pl.ds(r, ng, stride=SUBLANES), :] = sums[r] + prods[r] * h_in
        y_ref[0, :, lanes] = b_scr[hf] * _gelu(gate[:, lanes])


def _lru(pl_, cw, cb, wg, bg, lam, tile):
    b, lp, _ = pl_.shape
    vec = _const_spec((1, BRANCH_W))
    return pl.pallas_call(
        functools.partial(_lru_kernel, tl=tile),
        grid=(b, lp // tile),
        in_specs=[_row_spec(tile, LRU_W), _const_spec(cw.shape), vec, _const_spec(wg.shape),
                  _const_spec(bg.shape), vec],
        out_specs=_row_spec(tile, BRANCH_W),
        out_shape=jax.ShapeDtypeStruct((b, lp, BRANCH_W), F32),
        scratch_shapes=[pltpu.VMEM((SUBLANES, BRANCH_W), F32), pltpu.VMEM((SUBLANES, BRANCH_W), F32),
                        pltpu.VMEM((BRANCH_W // LANES, tile, LANES), F32),
                        pltpu.VMEM((BRANCH_W // LANES, tile, LANES), F32)],
        compiler_params=_params(("parallel", "arbitrary")),
        name="rglru",
    )(pl_, cw, cb, wg, bg, lam)


MERGE_COLS = 256


def _merge_kernel(h_ref, g_ref, u5_ref, y5_ref, yg_ref, yr_ref, yl_ref, d5_ref, wglu_ref, bglu_ref,
                  wgate_ref, bgate_ref, wbr_ref, wout_ref, o_ref, m_scr, *, tm):
    x = h_ref[0]
    xn = _rmsnorm_rows(x, g_ref[...]).astype(BF16)
    ys = _gelu(y5_ref[0] + d5_ref[...] * u5_ref[0])
    ys = ys * _sigmoid(_mm(ys, wglu_ref[...]) + bglu_ref[...])
    branches = tuple(yb.astype(BF16) for yb in (ys, yg_ref[0], yr_ref[0], yl_ref[0]))
    for c in range(D_MODEL // MERGE_COLS):
        cols = slice(c * MERGE_COLS, (c + 1) * MERGE_COLS)
        merged = None
        for i, yb in enumerate(branches):
            gate = _sigmoid(jnp.dot(xn, wgate_ref[i, :, cols], preferred_element_type=F32)
                            + bgate_ref[i, :, cols])
            term = gate * jnp.dot(yb, wbr_ref[i, :, cols], preferred_element_type=F32)
            merged = term if merged is None else merged + term
        m_scr[:, cols] = merged.astype(BF16)
    upd = jnp.dot(m_scr[...], wout_ref[...], preferred_element_type=F32)
    ridx = pl.program_id(1) * tm + _iota((tm, 1), 0)
    o_ref[0] = x + jnp.where(ridx >= PAD, upd, 0.0)


def _merge(h, g, u5, y5, yg, yr, yl, d5, wglu, bglu, wgate, bgate, wbr, wout, tm):
    b, lp, d = h.shape
    ybs = _row_spec(tm, BRANCH_W)
    return pl.pallas_call(
        functools.partial(_merge_kernel, tm=tm),
        grid=(b, lp // tm),
        in_specs=[_row_spec(tm, d), _const_spec((1, d)), ybs, ybs, ybs, ybs, ybs,
                  _const_spec((1, BRANCH_W)), _const_spec(wglu.shape), _const_spec((1, BRANCH_W)),
                  _const_spec(wgate.shape), _const_spec(bgate.shape), _const_spec(wbr.shape),
                  _const_spec(wout.shape)],
        out_specs=_row_spec(tm, d),
        out_shape=jax.ShapeDtypeStruct((b, lp, d), F32),
        scratch_shapes=[pltpu.VMEM((tm, d), BF16)],
        compiler_params=_params(("parallel", "parallel")),
        name="merge",
    )(h, g, u5, y5, yg, yr, yl, d5, wglu, bglu, wgate, bgate, wbr, wout)


FFN_COLS = 256


def _ffn_kernel(*refs, tm, final):
    if final:
        h_ref, g_ref, wup_ref, cw_ref, cb_ref, wdn_ref, gf_ref, o_ref, g_scr, act_scr = refs
    else:
        h_ref, g_ref, wup_ref, cw_ref, cb_ref, wdn_ref, o_ref, g_scr, act_scr = refs
    j = pl.program_id(1)

    @pl.when(j == 0)
    def _():
        g_scr[...] = jnp.zeros_like(g_scr)

    x = h_ref[0]
    xn = _rmsnorm_rows(x, g_ref[...]).astype(BF16)
    for c in range(D_FF // FFN_COLS):
        lo = c * FFN_COLS
        gate = jnp.dot(xn, wup_ref[:, lo:lo + FFN_COLS], preferred_element_type=F32)
        up = jnp.dot(xn, wup_ref[:, D_FF + lo:D_FF + lo + FFN_COLS], preferred_element_type=F32)
        carry8 = g_scr[:, lo:lo + FFN_COLS]
        g_scr[:, lo:lo + FFN_COLS] = gate[tm - SUBLANES:tm, :]
        gc = cb_ref[:, lo:lo + FFN_COLS] + cw_ref[FFN_CONV_W - 1:FFN_CONV_W, lo:lo + FFN_COLS] * gate
        for s in range(1, FFN_CONV_W):
            gc = gc + (cw_ref[FFN_CONV_W - 1 - s:FFN_CONV_W - s, lo:lo + FFN_COLS]
                       * _shift_rows(gate, s, carry8))
        act_scr[:, lo:lo + FFN_COLS] = (_silu(gc) * up).astype(BF16)
    acc = jnp.dot(act_scr[...], wdn_ref[...], preferred_element_type=F32)
    ridx = j * tm + _iota((tm, 1), 0)
    out = x + jnp.where(ridx >= PAD, acc, 0.0)
    if final:
        out = _rmsnorm_rows(out, gf_ref[...])
    o_ref[0] = out


def _ffn(h, g, wup, cw, cb, wdn, gf, tm):
    b, lp, d = h.shape
    final = gf is not None
    ins = [h, g, wup, cw, cb, wdn]
    specs = [_row_spec(tm, d), _const_spec((1, d)), _const_spec(wup.shape), _const_spec(cw.shape),
             _const_spec(cb.shape), _const_spec(wdn.shape)]
    if final:
        ins.append(gf)
        specs.append(_const_spec((1, d)))
    return pl.pallas_call(
        functools.partial(_ffn_kernel, tm=tm, final=final),
        grid=(b, lp // tm),
        in_specs=specs,
        out_specs=_row_spec(tm, d),
        out_shape=jax.ShapeDtypeStruct((b, lp, d), F32),
        scratch_shapes=[pltpu.VMEM((SUBLANES, D_FF), F32), pltpu.VMEM((tm, D_FF), BF16)],
        compiler_params=_params(("parallel", "arbitrary")),
        name="ffn",
    )(*ins)


def _pad_rows(m, rows, at=0):
    return jnp.zeros((rows, m.shape[1]), m.dtype).at[at:at + m.shape[0]].set(m)


def _block_diag(w):
    hn, n, _ = w.shape
    out = jnp.zeros((hn * n, hn * n), w.dtype)
    for i in range(hn):
        out = out.at[i * n:(i + 1) * n, i * n:(i + 1) * n].set(w[i])
    return out


def kernel(x, meta, w_in, rw_mu, s5_a_re, s5_a_im, s5_log_dt, s5_b_re, s5_b_im, s5_c_re, s5_c_im, s5_d, s5_w_glu, s5_b_glu, gla_w_gate, gla_b_gate, gla_norm, rw_w0, rw_w2, rw_a0, rw_a2, rw_g2, rw_k_k, rw_k_a, rw_r_k, rw_ln_w, rw_ln_b, rw_v0, rw_v1, rw_v2, lru_conv_w, lru_conv_b, lru_w_a, lru_b_a, lru_w_i, lru_b_i, lru_lam, w_branch, w_merge_gate, b_merge_gate, w_out, norm_mix, norm_ffn, ffn_w_up, ffn_conv_w, ffn_conv_b, ffn_w_down, norm_final):
    bsz, seq, d = x.shape
    assert d == D_MODEL and seq % CHUNK == 0
    lp = FRONT + seq
    tm = _pick_tile(lp, 1040, SUBLANES)
    tchunk = _pick_tile(lp, 832)
    tstream = _pick_tile(lp, STREAM_DEPTH * CHUNK)
    row = lambda v: v.astype(F32).reshape(1, -1)

    h = jnp.concatenate([jnp.zeros((bsz, PAD, d), x.dtype),
                         jnp.broadcast_to(meta.astype(x.dtype)[None], (bsz, N_META, d)), x], axis=1)
    v_first = None
    for l in range(DEPTH):
        wl = w_in[l]
        o_gla = BRANCH_W
        o_rw = o_gla + 784
        o_lru = o_rw + 896
        hq = GLA_HEADS * GLA_DK
        gla_cols = jnp.concatenate([
            wl[:, o_gla:o_gla + 2 * hq + BRANCH_W],
            wl[:, o_gla + 2 * hq + BRANCH_W + GLA_GATE_RANK:o_rw],
            wl[:, o_gla + 2 * hq + BRANCH_W:o_gla + 2 * hq + BRANCH_W + GLA_GATE_RANK],
            jnp.zeros((d, LANES - GLA_GATE_RANK), wl.dtype)], axis=1)
        rw_cols = wl[:, o_rw:o_lru]
        if l > 0:
            rw_cols = jnp.concatenate([rw_cols, rw_v1[l - 1],
                                       jnp.zeros((d, LANES - RW_V_RANK), wl.dtype)], axis=1)
        wcat = jnp.concatenate([wl[:, 0:BRANCH_W], gla_cols, rw_cols, wl[:, o_lru:]], axis=1).astype(BF16)
        widths = (BRANCH_W, GLA_W, RW_WV if l > 0 else RW_W, LRU_W)
        p5, pg, pr, pu = _proj(h, row(norm_mix[l]), wcat, widths, tm)

        y5 = _s5(p5, _s5_mats(s5_a_re[l], s5_a_im[l], s5_log_dt[l], s5_b_re[l], s5_b_im[l],
                              s5_c_re[l], s5_c_im[l]))
        yg = _gla(pg, _pad_rows(gla_w_gate[l].astype(F32), LANES), row(gla_b_gate[l]), row(gla_norm[l]), tstream)
        prm = dict(mu=row(rw_mu[l]), w0=row(rw_w0[l]), w2=_pad_rows(rw_w2[l].astype(F32), LANES, 0),
                   a0=row(rw_a0[l]), a2=_pad_rows(rw_a2[l].astype(F32), LANES, RW_W_RANK),
                   g2=_pad_rows(rw_g2[l].astype(F32), LANES, RW_W_RANK + RW_A_RANK),
                   k_k=row(rw_k_k[l]), k_a=row(rw_k_a[l]), r_k=row(rw_r_k[l]),
                   ln_w=row(rw_ln_w[l]), ln_b=row(rw_ln_b[l]))
        if l > 0:
            prm["v0"] = row(rw_v0[l - 1])
            prm["v2"] = _pad_rows(rw_v2[l - 1].astype(F32), LANES)
        yr, v_first = _rwkv(pr, v_first, prm, tstream)
        wgi = jnp.concatenate([_block_diag(lru_w_a[l]), _block_diag(lru_w_i[l])], axis=1).astype(BF16)
        bgi = jnp.concatenate([lru_b_a[l], lru_b_i[l]]).astype(F32).reshape(1, -1)
        yl = _lru(pu, lru_conv_w[l].astype(F32), row(lru_conv_b[l]), wgi, bgi, row(lru_lam[l]), tchunk)

        h = _merge(h, row(norm_mix[l]), p5, y5, yg, yr, yl, row(s5_d[l]), s5_w_glu[l].astype(BF16),
                   row(s5_b_glu[l]), w_merge_gate[l].astype(BF16),
                   b_merge_gate[l].astype(F32).reshape(4, 1, d), w_branch[l].astype(BF16),
                   w_out[l].astype(BF16), tm)
        h = _ffn(h, row(norm_ffn[l]), ffn_w_up[l].astype(BF16), ffn_conv_w[l].astype(F32),
                 row(ffn_conv_b[l]), ffn_w_down[l].astype(BF16),
                 row(norm_final) if l == DEPTH - 1 else None, tm)
    return h[:, FRONT:]
```

```python
import functools
import math

import numpy as np
import jax
import jax.numpy as jnp
from jax import lax
from jax.experimental import pallas as pl
from jax.experimental.pallas import tpu as pltpu

F32 = jnp.float32
BF16 = jnp.bfloat16

D_MODEL = 1024
DEPTH = 4
N_META = 16
BRANCH_W = 256
S5_GROUP = 16
S5_GROUPS = 16
S5_STATE = 64
S5_BLOCK = 16
GLA_HEADS = 4
GLA_DK = 32
GLA_DV = 64
GLA_GATE_RANK = 16
GLA_TAU = 16.0
CHUNK = 64
RW_HEADS = 4
RW_HEAD = 64
RW_W_RANK = 32
RW_A_RANK = 32
RW_V_RANK = 16
RW_G_RANK = 64
RW_LN_EPS = 64e-5
LRU_C = 8.0
LRU_CONV_W = 4
D_FF = 2816
FFN_CONV_W = 3
RMS_EPS = 1e-6

PAD = (-N_META) % CHUNK
FRONT = PAD + N_META
LANES = 128
SUBLANES = 8
VMEM_LIMIT = 56 * 1024 * 1024

GLA_W = 896
RW_W = 896
RW_WV = 1024
LRU_W = 512


def _mm(a, b):
    return jnp.dot(a.astype(BF16), b.astype(BF16), preferred_element_type=F32)


def _mm_nt(a, b):
    return lax.dot_general(a.astype(BF16), b.astype(BF16), (((1,), (1,)), ((), ())),
                           preferred_element_type=F32)


def _mm_tn(a, b):
    return lax.dot_general(a.astype(BF16), b.astype(BF16), (((0,), (0,)), ((), ())),
                           preferred_element_type=F32)


def _split2(a):
    hi = a.astype(BF16)
    lo = (a - hi.astype(F32)).astype(BF16)
    return hi, lo


def _split3(a):
    hi = a.astype(BF16)
    r1 = a - hi.astype(F32)
    mid = r1.astype(BF16)
    lo = (r1 - mid.astype(F32)).astype(BF16)
    return hi, mid, lo


def _dg(a, b, dims):
    return lax.dot_general(a, b, (dims, ((), ())), preferred_element_type=F32)


def _mm3_dims(a, b, dims):
    ah, al = _split2(a)
    bh, bl = _split2(b)
    return _dg(ah, bh, dims) + (_dg(ah, bl, dims) + _dg(al, bh, dims))


_NN = ((1,), (0,))
_NT = ((1,), (1,))
_TN = ((0,), (0,))


def _mm3(a, b):
    return _mm3_dims(a, b, _NN)


def _mm3_nt(a, b):
    return _mm3_dims(a, b, _NT)


def _mm3_tn(a, b):
    return _mm3_dims(a, b, _TN)


def _mm_exact_rhs(a, b_bf16):
    hi, mid, lo = _split3(a)
    return _dg(hi, b_bf16, _NN) + (_dg(mid, b_bf16, _NN) + _dg(lo, b_bf16, _NN))


def _mm_exact_lhs(a_bf16, b):
    hi, mid, lo = _split3(b)
    return _dg(a_bf16, hi, _NN) + (_dg(a_bf16, mid, _NN) + _dg(a_bf16, lo, _NN))


def _softplus(x):
    return jnp.maximum(x, 0.0) + jnp.log1p(jnp.exp(-jnp.abs(x)))


def _sigmoid(x):
    return jax.nn.sigmoid(x)


def _silu(x):
    return x * jax.nn.sigmoid(x)


def _gelu(x):
    return jax.nn.gelu(x)


def _rmsnorm_rows(x, g):
    ms = jnp.mean(x * x, axis=-1, keepdims=True)
    return x * lax.rsqrt(ms + RMS_EPS) * g


def _iota(shape, dim):
    return lax.broadcasted_iota(jnp.int32, shape, dim)


def _div(x, n):
    return x >> int(math.log2(n))


def _mod(x, n):
    return x & (n - 1)


def _expand_heads(x, block_mask):
    return jnp.where(block_mask, _tile_rows4(x), 0.0)


def _tile_rows4(x):
    return jnp.concatenate([x, x, x, x], axis=0)


def _fold_rows4(z):
    n = z.shape[0] // 4
    return (z[0:n] + z[n:2 * n]) + (z[2 * n:3 * n] + z[3 * n:4 * n])


def _shift_rows(x, s, carry8):
    xs = pltpu.roll(x, s, axis=0)
    top = jnp.where(_iota((SUBLANES, x.shape[1]), 0) < s,
                    pltpu.roll(carry8, s, axis=0), xs[0:SUBLANES])
    return jnp.concatenate([top, xs[SUBLANES:]], axis=0)


def _lockstep(gens, skew):
    gens = list(gens)
    slot = 0
    live = list(range(len(gens)))
    while live:
        nxt = []
        for i in live:
            if slot < i * skew:
                nxt.append(i)
                continue
            try:
                next(gens[i])
                nxt.append(i)
            except StopIteration:
                pass
        live = nxt
        slot += 1


def _loop_streams(n, rows, chunk_gen, depth, skew):
    def trip(i, carry):
        _lockstep([chunk_gen(r, depth * i + d) for d in range(depth) for r in range(rows)], skew)
        return carry

    if n >= depth:
        lax.fori_loop(0, n // depth, trip, 0)
    tail = n % depth
    if tail:
        _lockstep([chunk_gen(r, n - tail + d) for d in range(tail) for r in range(rows)], skew)


def _chunk_rows(c):
    start = c * CHUNK
    return pl.ds(start if isinstance(c, int) else pl.multiple_of(start, CHUNK), CHUNK)


def _const_spec(shape):
    nd = len(shape)
    return pl.BlockSpec(shape, lambda *_: (0,) * nd, pipeline_mode=pl.Buffered(1))


def _row_spec(rows, width, nb=1):
    return pl.BlockSpec((nb, rows, width), lambda b, j: (b, j, 0))


STREAM_ROWS = 2
STREAM_DEPTH = 5
RW_SKEW = 1
GLA_SKEW = 0


def _stream_rows(b):
    return STREAM_ROWS if b % STREAM_ROWS == 0 else 1


def _params(sem):
    return pltpu.CompilerParams(dimension_semantics=sem, vmem_limit_bytes=VMEM_LIMIT)


def _pick_tile(lp, cap, step=CHUNK):
    best = step
    t = step
    while t <= min(lp, cap):
        if lp % t == 0:
            best = t
        t += step
    return best


def _proj_kernel(h_ref, g_ref, w_ref, *out_refs, widths):
    xn = _rmsnorm_rows(h_ref[0], g_ref[...]).astype(BF16)
    off = 0
    for o_ref, w in zip(out_refs, widths):
        o_ref[0] = jnp.dot(xn, w_ref[:, off:off + w], preferred_element_type=F32)
        off += w


def _proj(h, g, w, widths, tm):
    b, lp, d = h.shape
    nt = lp // tm
    return pl.pallas_call(
        functools.partial(_proj_kernel, widths=widths),
        grid=(b, nt),
        in_specs=[_row_spec(tm, d), _const_spec((1, d)), _const_spec(w.shape)],
        out_specs=[_row_spec(tm, wd) for wd in widths],
        out_shape=[jax.ShapeDtypeStruct((b, lp, wd), F32) for wd in widths],
        compiler_params=_params(("parallel", "parallel")),
        name="proj",
    )(h, g, w)


def _cmul(pr, pi, zr, zi):
    return pr * zr - pi * zi, pr * zi + pi * zr


def _s5_kernel(u_ref, kd_ref, bb_ref, cc_ref, pe_ref, pc_ref, a_ref, y_ref,
               io_scr, xs_scr, ys_scr, e_scr, sp_scr, *, nb, nbp):
    ns = S5_GROUPS * S5_STATE
    pad = nbp - nb
    halves = BRANCH_W // LANES
    for hf in range(halves):
        io_scr[hf] = u_ref[0, :, hf * LANES:(hf + 1) * LANES]
    for s in range(S5_BLOCK):
        for hf in range(halves):
            xs_scr[s, 0:nb, hf * LANES:(hf + 1) * LANES] = (
                io_scr[hf, pl.ds(s, nb, stride=S5_BLOCK), :].astype(BF16))
        if pad:
            xs_scr[s, nb:nbp, :] = jnp.zeros((pad, BRANCH_W), BF16)
    if pad:
        sp_scr[nb:nbp, :] = jnp.zeros((pad, 2 * ns), F32)

    e_re = None
    for s in range(S5_BLOCK):
        z = jnp.dot(xs_scr[s], bb_ref[...], preferred_element_type=F32)
        zr, zi = _cmul(pe_ref[s:s + 1, 0:ns], pe_ref[s:s + 1, ns:], z[:, 0:ns], z[:, ns:])
        e_re, e_im = (zr, zi) if e_re is None else (e_re + zr, e_im + zi)
    e_scr[:, 0:ns] = e_re
    e_scr[:, ns:] = e_im

    ar = a_ref[0:1, 0:ns]
    ai = a_ref[0:1, ns:]

    def step(k, carry):
        sr, si = carry
        sp_scr[pl.ds(k, 1), 0:ns] = sr
        sp_scr[pl.ds(k, 1), ns:] = si
        nr, ni = _cmul(ar, ai, sr, si)
        return nr + e_scr[pl.ds(k, 1), 0:ns], ni + e_scr[pl.ds(k, 1), ns:]

    zero = jnp.zeros((1, ns), F32)
    lax.fori_loop(0, nb, step, (zero, zero))

    for d in range(S5_BLOCK):
        m = S5_BLOCK - d
        prod = jnp.dot(xs_scr[0:m].reshape(m * nbp, BRANCH_W), kd_ref[d], preferred_element_type=F32)
        prod = prod.reshape(m, nbp, BRANCH_W)
        if d == 0:
            ys_scr[...] = prod
        else:
            ys_scr[d:S5_BLOCK] = ys_scr[d:S5_BLOCK] + prod

    spr = sp_scr[:, 0:ns]
    spi = sp_scr[:, ns:]
    for t in range(S5_BLOCK):
        gr, gi = _cmul(pc_ref[t:t + 1, 0:ns], pc_ref[t:t + 1, ns:], spr, spi)
        g = jnp.concatenate([gr, gi], axis=1).astype(BF16)
        yt = ys_scr[t] + jnp.dot(g, cc_ref[...], preferred_element_type=F32)
        for hf in range(halves):
            io_scr[hf, pl.ds(t, nb, stride=S5_BLOCK), :] = yt[0:nb, hf * LANES:(hf + 1) * LANES]
    for hf in range(halves):
        y_ref[0, :, hf * LANES:(hf + 1) * LANES] = io_scr[hf]


def _s5_mats(a_re, a_im, log_dt, b_re, b_im, c_re, c_im):
    lr, li = a_re.astype(F32), a_im.astype(F32)
    dt = jnp.exp(log_dt.astype(F32))[:, None]
    mag = jnp.exp(lr * dt)
    abr, abi = mag * jnp.cos(li * dt), mag * jnp.sin(li * dt)
    den = lr * lr + li * li
    cr = ((abr - 1.0) * lr + abi * li) / den
    ci = (abi * lr - (abr - 1.0) * li) / den
    d = jnp.arange(S5_BLOCK + 1, dtype=F32)[:, None, None]
    pmag = jnp.exp(lr * dt * d)
    p_re, p_im = pmag * jnp.cos(li * dt * d), pmag * jnp.sin(li * dt * d)
    bb_re = cr[..., None] * b_re - ci[..., None] * b_im
    bb_im = cr[..., None] * b_im + ci[..., None] * b_re
    cp_re = c_re[None] * p_re[:, :, None, :] - c_im[None] * p_im[:, :, None, :]
    cp_im = c_re[None] * p_im[:, :, None, :] + c_im[None] * p_re[:, :, None, :]
    hp = lax.Precision.HIGHEST
    kd = (jnp.einsum("dgon,gni->dgoi", cp_re, bb_re, precision=hp)
          - jnp.einsum("dgon,gni->dgoi", cp_im, bb_im, precision=hp))
    eye_g = jnp.eye(S5_GROUPS, dtype=F32)
    kmat = jnp.einsum("dgoi,gq->dgiqo", kd[:S5_BLOCK], eye_g).reshape(S5_BLOCK, BRANCH_W, BRANCH_W)
    ns = S5_GROUPS * S5_STATE
    bmat = jnp.concatenate([jnp.einsum("gni,gq->giqn", bb_re, eye_g).reshape(BRANCH_W, ns),
                            jnp.einsum("gni,gq->giqn", bb_im, eye_g).reshape(BRANCH_W, ns)], axis=1)
    cmat = jnp.concatenate([jnp.einsum("gon,gq->gnqo", c_re.astype(F32), eye_g).reshape(ns, BRANCH_W),
                            jnp.einsum("gon,gq->gnqo", -c_im.astype(F32), eye_g).reshape(ns, BRANCH_W)], axis=0)
    flat = lambda z: z.reshape(z.shape[0], ns)
    t = np.arange(S5_BLOCK)
    pe = jnp.concatenate([flat(p_re[S5_BLOCK - 1 - t]), flat(p_im[S5_BLOCK - 1 - t])], axis=1)
    pc = jnp.concatenate([flat(p_re[1:]), flat(p_im[1:])], axis=1)
    a16 = jnp.concatenate([flat(p_re[S5_BLOCK:]), flat(p_im[S5_BLOCK:])], axis=1)
    a16 = jnp.pad(a16, ((0, SUBLANES - 1), (0, 0)))
    return kmat.astype(BF16), bmat.astype(BF16), cmat.astype(BF16), pe, pc, a16


def _s5(p5, mats):
    kmat, bmat, cmat, pe, pc, a16 = mats
    b, lp, _ = p5.shape
    nb = lp // S5_BLOCK
    nbp = -(-nb // 16) * 16
    ns2 = 2 * S5_GROUPS * S5_STATE
    return pl.pallas_call(
        functools.partial(_s5_kernel, nb=nb, nbp=nbp),
        grid=(b,),
        in_specs=[pl.BlockSpec((1, lp, BRANCH_W), lambda i: (i, 0, 0)), _const_spec(kmat.shape),
                  _const_spec(bmat.shape), _const_spec(cmat.shape), _const_spec(pe.shape),
                  _const_spec(pc.shape), _const_spec(a16.shape)],
        out_specs=pl.BlockSpec((1, lp, BRANCH_W), lambda i: (i, 0, 0)),
        out_shape=jax.ShapeDtypeStruct((b, lp, BRANCH_W), F32),
        scratch_shapes=[pltpu.VMEM((BRANCH_W // LANES, lp, LANES), F32),
                        pltpu.VMEM((S5_BLOCK, nbp, BRANCH_W), BF16),
                        pltpu.VMEM((S5_BLOCK, nbp, BRANCH_W), F32),
                        pltpu.VMEM((nbp, ns2), F32),
                        pltpu.VMEM((nbp, ns2), F32)],
        compiler_params=_params(("parallel",)),
        name="s5",
    )(p5, kmat, bmat, cmat, pe, pc, a16)


def _gla_kernel(p_ref, wg_ref, bg_ref, nw_ref, y_ref, st_scr, *, cpt, nrows):
    @pl.when(pl.program_id(1) == 0)
    def _():
        st_scr[...] = jnp.zeros_like(st_scr)

    hq = GLA_HEADS * GLA_DK
    hv = GLA_HEADS * GLA_DV
    rows4 = GLA_HEADS * CHUNK
    tri = (_iota((CHUNK, CHUNK), 0) >= _iota((CHUNK, CHUNK), 1)).astype(BF16)
    bd_k = _div(_iota((rows4, hq), 0), CHUNK) == _div(_iota((rows4, hq), 1), GLA_DK)
    causal = _mod(_iota((CHUNK, rows4), 1), CHUNK) <= _iota((CHUNK, rows4), 0)
    bd_v = _div(_iota((rows4, hv), 0), CHUNK) == _div(_iota((rows4, hv), 1), GLA_DV)
    bd_st = _div(_iota((hv, hq), 0), GLA_DV) == _div(_iota((hv, hq), 1), GLA_DK)
    head_ones = (_div(_iota((hv, hv), 0), GLA_DV) == _div(_iota((hv, hv), 1), GLA_DV)).astype(BF16)
    wg = wg_ref[...]
    bg = bg_ref[...]
    nw = nw_ref[...]

    def chunk(bi, c):
        rows = _chunk_rows(c)
        p = p_ref[bi, rows, :]
        q = p[:, 0:hq]
        k = p[:, hq:2 * hq]
        v = p[:, 2 * hq:2 * hq + hv]
        og = p[:, 2 * hq + hv:2 * hq + 2 * hv]
        gd = p[:, 2 * hq + 2 * hv:]
        x = _mm3(gd, wg) + bg
        yield
        g = (jnp.minimum(x, 0.0) - jnp.log1p(jnp.exp(-jnp.abs(x)))) / GLA_TAU
        gc = _mm_exact_lhs(tri, g)
        yield
        gl = gc[CHUNK - 1:CHUNK, :]
        qg = q * (GLA_DK ** -0.5) * jnp.exp(gc)
        kg = k * jnp.exp(-gc)
        kl = k * jnp.exp(gl - gc)
        kg_e = _expand_heads(kg, bd_k)
        att = _mm_nt(qg, kg_e)
        kv = _mm_tn(v, kl)
        yield
        att = jnp.where(causal, att, 0.0)
        v_e = _expand_heads(v, bd_v)
        st = st_scr[bi]
        o = _mm(att, v_e) + _mm_nt(qg, st)
        st_scr[bi] = st * jnp.exp(gl) + jnp.where(bd_st, kv, 0.0)
        yield
        ms = _mm(o * o, head_ones) * (1.0 / GLA_DV)
        yield
        o = o * lax.rsqrt(ms + RMS_EPS) * nw
        y_ref[bi, rows, :] = o * _silu(og)

    _loop_streams(cpt, nrows, chunk, STREAM_DEPTH, GLA_SKEW)


def _gla(pg, w_gate_pad, b_gate, norm_w, tile):
    b, lp, _ = pg.shape
    cpt = tile // CHUNK
    nrows = _stream_rows(b)
    return pl.pallas_call(
        functools.partial(_gla_kernel, cpt=cpt, nrows=nrows),
        grid=(b // nrows, lp // tile),
        in_specs=[_row_spec(tile, GLA_W, nrows), _const_spec(w_gate_pad.shape),
                  _const_spec(b_gate.shape), _const_spec(norm_w.shape)],
        out_specs=_row_spec(tile, BRANCH_W, nrows),
        out_shape=jax.ShapeDtypeStruct((b, lp, BRANCH_W), F32),
        scratch_shapes=[pltpu.VMEM((nrows, GLA_HEADS * GLA_DV, GLA_HEADS * GLA_DK), F32)],
        compiler_params=_params(("parallel", "arbitrary")),
        name="gla",
    )(pg, w_gate_pad, b_gate, norm_w)


def _rw_kernel(*refs, cpt, nrows, has_vgate):
    if has_vgate:
        (p_ref, vf_ref, mu_ref, w0_ref, w2_ref, a0_ref, a2_ref, g2_ref, kk_ref, ka_ref, rk_ref,
         lnw_ref, lnb_ref, v0_ref, v2_ref, y_ref, carry_scr, h_scr) = refs
    else:
        (p_ref, mu_ref, w0_ref, w2_ref, a0_ref, a2_ref, g2_ref, kk_ref, ka_ref, rk_ref,
         lnw_ref, lnb_ref, y_ref, vf_out_ref, carry_scr, h_scr) = refs

    @pl.when(pl.program_id(1) == 0)
    def _():
        carry_scr[...] = jnp.zeros_like(carry_scr)
        h_scr[...] = jnp.zeros_like(h_scr)

    w = BRANCH_W
    rows4 = RW_HEADS * CHUNK
    r_i = _iota((rows4, w), 0)
    c_i = _iota((rows4, w), 1)
    bd = _div(r_i, CHUNK) == _div(c_i, RW_HEAD)
    eye = (r_i == c_i).astype(F32)
    t_row = _iota((CHUNK, w), 0)
    s_col = _mod(_iota((CHUNK, w), 1), CHUNK)
    strict = s_col < t_row
    incl = s_col <= t_row
    eye_c = (s_col == t_row).astype(F32)
    tri = (_iota((CHUNK, CHUNK), 0) >= _iota((CHUNK, CHUNK), 1)).astype(BF16)
    head_ones = bd.astype(BF16)
    row0 = _iota((CHUNK, RW_W), 0) == 0
    zeros_cw = jnp.zeros((CHUNK, w), F32)
    mu = mu_ref[...]
    inv_n = 1.0 / RW_HEAD

    def expand(x):
        return _expand_heads(x, bd)

    def headsum(x):
        return _mm(x, head_ones)

    def chunk(bi, c):
        rows = _chunk_rows(c)
        p = p_ref[bi, rows, :]
        pf = p[:, 0:RW_W]
        prev = jnp.where(row0, carry_scr[bi, 0:1, :], pltpu.roll(pf, 1, axis=0))
        carry_scr[bi, 0:1, :] = pf[CHUNK - 1:CHUNK, :]
        xs = pf + mu * (prev - pf)
        r = xs[:, 0:w]
        k = xs[:, w:2 * w]
        v = xs[:, 2 * w:3 * w]
        lr = xs[:, 3 * w:]
        wlow = _mm3(jnp.tanh(lr), w2_ref[...])
        alow = _mm(lr, a2_ref[...])
        g = _mm(_sigmoid(lr), g2_ref[...])
        if has_vgate:
            vlow = _mm(p[:, RW_W:], v2_ref[...])
        kk = k * kk_ref[...]
        kss = headsum(kk * kk)
        yield
        w_log = -_softplus(-(w0_ref[...] + wlow)) - 0.5
        lw = -jnp.exp(w_log)
        cum = _mm_exact_lhs(tri, lw)
        ag = _sigmoid(a0_ref[...] + alow)
        if has_vgate:
            v = v + (vf_ref[bi, rows, :] - v) * _sigmoid(v0_ref[...] + vlow)
        else:
            vf_out_ref[bi, rows, :] = v
        kk = kk * lax.rsqrt(jnp.maximum(kss, 1e-24))
        k2 = k * (1.0 + (ag - 1.0) * ka_ref[...])
        bvec = kk * ag
        rk_sum = headsum(r * k2 * rk_ref[...])
        v_e = expand(v)
        yield
        tot = cum[CHUNK - 1:CHUNK, :]
        e_neg = jnp.exp(-cum)
        e_tail = jnp.exp(tot - cum)
        a_n = -kk * jnp.exp(cum - lw)
        r_n = r * jnp.exp(cum)
        lhs = jnp.concatenate([a_n, r_n], axis=0)
        rhs = jnp.concatenate([expand(bvec * e_neg), expand(k2 * e_neg)], axis=0)
        aa = _mm_nt(lhs, rhs)
        yield
        a_ab = jnp.where(strict, aa[0:CHUNK, 0:w], 0.0)
        a_ak = jnp.where(strict, aa[0:CHUNK, w:], 0.0)
        a_rb = jnp.where(incl, aa[CHUNK:, 0:w], 0.0)
        a_rk = jnp.where(incl, aa[CHUNK:, w:], 0.0)
        tinv = eye_c + a_ab
        pw = a_ab
        pw_e = expand(pw)
        akv = _mm(a_ak, v_e)
        rkv = _mm(a_rk, v_e)
        for _ in range(5):
            pw = _mm(pw, pw_e)
            yield
            pw_e = expand(pw)
            tinv = tinv + _mm(tinv, pw_e)
        yield
        x_n = _mm(tinv, jnp.concatenate([expand(a_n), expand(akv)], axis=1))
        yield
        yq = _mm(a_rb, jnp.concatenate([expand(x_n[:, 0:w]), expand(x_n[:, w:])], axis=1))
        mn = _mm_tn(jnp.concatenate([bvec * e_tail, k2 * e_tail], axis=0),
                    jnp.concatenate([x_n, jnp.concatenate([zeros_cw, v], axis=1)], axis=0))
        yield
        q_n = r_n + yq[:, 0:w]
        ol_n = yq[:, w:] + rkv
        m_mat = jnp.where(bd, mn[:, 0:w], 0.0) + eye * jnp.exp(tot)
        n_mat = jnp.where(bd, mn[:, w:], 0.0)
        h = h_scr[bi]
        o = ol_n + _mm(q_n, h)
        h_scr[bi] = _mm(m_mat, h) + n_mat
        yield
        mean = headsum(o) * inv_n
        yield
        dlt = o - mean
        var = headsum(dlt * dlt) * inv_n
        yield
        o = dlt * lax.rsqrt(var + RW_LN_EPS) * lnw_ref[...] + lnb_ref[...]
        y_ref[bi, rows, :] = (o + rk_sum * v) * g

    _loop_streams(cpt, nrows, chunk, STREAM_DEPTH, RW_SKEW)


def _rwkv(pr, v_first, prm, tile):
    b, lp, pw = pr.shape
    has_vgate = v_first is not None
    cpt = tile // CHUNK
    vec = _const_spec((1, BRANCH_W))
    mat = _const_spec((LANES, BRANCH_W))
    nrows = _stream_rows(b)
    ins = [pr]
    specs = [_row_spec(tile, pw, nrows)]
    if has_vgate:
        ins.append(v_first)
        specs.append(_row_spec(tile, BRANCH_W, nrows))
    ins += [prm["mu"], prm["w0"], prm["w2"], prm["a0"], prm["a2"], prm["g2"], prm["k_k"], prm["k_a"],
            prm["r_k"], prm["ln_w"], prm["ln_b"]]
    specs += [_const_spec((1, RW_W)), vec, mat, vec, mat, mat, vec, vec, vec, vec, vec]
    out_shape = [jax.ShapeDtypeStruct((b, lp, BRANCH_W), F32)]
    out_specs = [_row_spec(tile, BRANCH_W, nrows)]
    if has_vgate:
        ins += [prm["v0"], prm["v2"]]
        specs += [vec, mat]
    else:
        out_shape.append(jax.ShapeDtypeStruct((b, lp, BRANCH_W), F32))
        out_specs.append(_row_spec(tile, BRANCH_W, nrows))
    outs = pl.pallas_call(
        functools.partial(_rw_kernel, cpt=cpt, nrows=nrows, has_vgate=has_vgate),
        grid=(b // nrows, lp // tile),
        in_specs=specs,
        out_specs=out_specs,
        out_shape=out_shape,
        scratch_shapes=[pltpu.VMEM((nrows, SUBLANES, RW_W), F32),
                        pltpu.VMEM((nrows, RW_HEADS * RW_HEAD, BRANCH_W), F32)],
        compiler_params=_params(("parallel", "arbitrary")),
        name="rwkv7",
    )(*ins)
    if has_vgate:
        return outs[0], v_first
    return outs[0], outs[1]


def _lru_kernel(p_ref, cw_ref, cb_ref, wg_ref, bg_ref, lam_ref, y_ref, x_scr, h_scr, a_scr, b_scr, *, tl):
    j = pl.program_id(1)

    @pl.when(j == 0)
    def _():
        x_scr[...] = jnp.zeros_like(x_scr)
        h_scr[...] = jnp.zeros_like(h_scr)

    w = BRANCH_W
    p = p_ref[0]
    xr = p[:, 0:w]
    gate = p[:, w:]
    carry8 = x_scr[...]
    xc = cb_ref[...] + cw_ref[LRU_CONV_W - 1:LRU_CONV_W, :] * xr
    for s in range(1, LRU_CONV_W):
        xc = xc + cw_ref[LRU_CONV_W - 1 - s:LRU_CONV_W - s, :] * _shift_rows(xr, s, carry8)
    x_scr[...] = xr[tl - SUBLANES:tl, :]
    gates = _mm(xc, wg_ref[...]) + bg_ref[...]
    rg = _sigmoid(gates[:, 0:w])
    ig = _sigmoid(gates[:, w:])
    log_a = -LRU_C * rg * _softplus(-lam_ref[...])
    a = jnp.exp(log_a)
    mult = jnp.sqrt(1.0 - jnp.exp(2.0 * log_a))
    ridx = j * tl + _iota((tl, w), 0)
    mult = jnp.where(ridx == PAD, 1.0, mult)
    bterm = jnp.where(ridx >= PAD, mult * ig * xc, 0.0)

    ng = tl // SUBLANES
    g_i = _iota((ng, LANES), 0)
    for hf in range(w // LANES):
        lanes = slice(hf * LANES, (hf + 1) * LANES)
        a_scr[hf] = a[:, lanes]
        b_scr[hf] = bterm[:, lanes]
        prods, sums = [], []
        for r in range(SUBLANES):
            a_r = a_scr[hf, pl.ds(r, ng, stride=SUBLANES), :]
            b_r = b_scr[hf, pl.ds(r, ng, stride=SUBLANES), :]
            prods.append(a_r if r == 0 else a_r * prods[-1])
            sums.append(b_r if r == 0 else a_r * sums[-1] + b_r)
        pt, ct = prods[-1], sums[-1]
        d = 1
        while d < ng:
            keep = g_i >= d
            ct = ct + pt * jnp.where(keep, pltpu.roll(ct, d, axis=0), 0.0)
            pt = pt * jnp.where(keep, pltpu.roll(pt, d, axis=0), 1.0)
            d *= 2
        h_end = ct + pt * h_scr[0:1, lanes]
        h_in = jnp.where(g_i == 0, h_scr[0:1, lanes], pltpu.roll(h_end, 1, axis=0))
        h_scr[0:1, lanes] = h_end[ng - 1:ng, :]
        for r in range(SUBLANES):
            b_scr[hf, pl.ds(r, ng, stride=SUBLANES), :] = sums[r] + prods[r] * h_in
        y_ref[0, :, lanes] = b_scr[hf] * _gelu(gate[:, lanes])


def _lru(pl_, cw, cb, wg, bg, lam, tile):
    b, lp, _ = pl_.shape
    vec = _const_spec((1, BRANCH_W))
    return pl.pallas_call(
        functools.partial(_lru_kernel, tl=tile),
        grid=(b, lp // tile),
        in_specs=[_row_spec(tile, LRU_W), _const_spec(cw.shape), vec, _const_spec(wg.shape),
                  _const_spec(bg.shape), vec],
        out_specs=_row_spec(tile, BRANCH_W),
        out_shape=jax.ShapeDtypeStruct((b, lp, BRANCH_W), F32),
        scratch_shapes=[pltpu.VMEM((SUBLANES, BRANCH_W), F32), pltpu.VMEM((SUBLANES, BRANCH_W), F32),
                        pltpu.VMEM((BRANCH_W // LANES, tile, LANES), F32),
                        pltpu.VMEM((BRANCH_W // LANES, tile, LANES), F32)],
        compiler_params=_params(("parallel", "arbitrary")),
        name="rglru",
    )(pl_, cw, cb, wg, bg, lam)


MERGE_COLS = 256


def _merge_kernel(h_ref, g_ref, u5_ref, y5_ref, yg_ref, yr_ref, yl_ref, d5_ref, wglu_ref, bglu_ref,
                  wgate_ref, bgate_ref, wbr_ref, wout_ref, o_ref, m_scr, *, tm):
    x = h_ref[0]
    xn = _rmsnorm_rows(x, g_ref[...]).astype(BF16)
    ys = _gelu(y5_ref[0] + d5_ref[...] * u5_ref[0])
    ys = ys * _sigmoid(_mm(ys, wglu_ref[...]) + bglu_ref[...])
    branches = tuple(yb.astype(BF16) for yb in (ys, yg_ref[0], yr_ref[0], yl_ref[0]))
    for c in range(D_MODEL // MERGE_COLS):
        cols = slice(c * MERGE_COLS, (c + 1) * MERGE_COLS)
        merged = None
        for i, yb in enumerate(branches):
            gate = _sigmoid(jnp.dot(xn, wgate_ref[i, :, cols], preferred_element_type=F32)
                            + bgate_ref[i, :, cols])
            term = gate * jnp.dot(yb, wbr_ref[i, :, cols], preferred_element_type=F32)
            merged = term if merged is None else merged + term
        m_scr[:, cols] = merged.astype(BF16)
    upd = jnp.dot(m_scr[...], wout_ref[...], preferred_element_type=F32)
    ridx = pl.program_id(1) * tm + _iota((tm, 1), 0)
    o_ref[0] = x + jnp.where(ridx >= PAD, upd, 0.0)


def _merge(h, g, u5, y5, yg, yr, yl, d5, wglu, bglu, wgate, bgate, wbr, wout, tm):
    b, lp, d = h.shape
    ybs = _row_spec(tm, BRANCH_W)
    return pl.pallas_call(
        functools.partial(_merge_kernel, tm=tm),
        grid=(b, lp // tm),
        in_specs=[_row_spec(tm, d), _const_spec((1, d)), ybs, ybs, ybs, ybs, ybs,
                  _const_spec((1, BRANCH_W)), _const_spec(wglu.shape), _const_spec((1, BRANCH_W)),
                  _const_spec(wgate.shape), _const_spec(bgate.shape), _const_spec(wbr.shape),
                  _const_spec(wout.shape)],
        out_specs=_row_spec(tm, d),
        out_shape=jax.ShapeDtypeStruct((b, lp, d), F32),
        scratch_shapes=[pltpu.VMEM((tm, d), BF16)],
        compiler_params=_params(("parallel", "parallel")),
        name="merge",
    )(h, g, u5, y5, yg, yr, yl, d5, wglu, bglu, wgate, bgate, wbr, wout)


FFN_COLS = 256


def _ffn_kernel(*refs, tm, final):
    if final:
        h_ref, g_ref, wup_ref, cw_ref, cb_ref, wdn_ref, gf_ref, o_ref, g_scr, act_scr = refs
    else:
        h_ref, g_ref, wup_ref, cw_ref, cb_ref, wdn_ref, o_ref, g_scr, act_scr = refs
    j = pl.program_id(1)

    @pl.when(j == 0)
    def _():
        g_scr[...] = jnp.zeros_like(g_scr)

    x = h_ref[0]
    xn = _rmsnorm_rows(x, g_ref[...]).astype(BF16)
    for c in range(D_FF // FFN_COLS):
        lo = c * FFN_COLS
        gate = jnp.dot(xn, wup_ref[:, lo:lo + FFN_COLS], preferred_element_type=F32)
        up = jnp.dot(xn, wup_ref[:, D_FF + lo:D_FF + lo + FFN_COLS], preferred_element_type=F32)
        carry8 = g_scr[:, lo:lo + FFN_COLS]
        g_scr[:, lo:lo + FFN_COLS] = gate[tm - SUBLANES:tm, :]
        gc = cb_ref[:, lo:lo + FFN_COLS] + cw_ref[FFN_CONV_W - 1:FFN_CONV_W, lo:lo + FFN_COLS] * gate
        for s in range(1, FFN_CONV_W):
            gc = gc + (cw_ref[FFN_CONV_W - 1 - s:FFN_CONV_W - s, lo:lo + FFN_COLS]
                       * _shift_rows(gate, s, carry8))
        act_scr[:, lo:lo + FFN_COLS] = (_silu(gc) * up).astype(BF16)
    acc = jnp.dot(act_scr[...], wdn_ref[...], preferred_element_type=F32)
    ridx = j * tm + _iota((tm, 1), 0)
    out = x + jnp.where(ridx >= PAD, acc, 0.0)
    if final:
        out = _rmsnorm_rows(out, gf_ref[...])
    o_ref[0] = out


def _ffn(h, g, wup, cw, cb, wdn, gf, tm):
    b, lp, d = h.shape
    final = gf is not None
    ins = [h, g, wup, cw, cb, wdn]
    specs = [_row_spec(tm, d), _const_spec((1, d)), _const_spec(wup.shape), _const_spec(cw.shape),
             _const_spec(cb.shape), _const_spec(wdn.shape)]
    if final:
        ins.append(gf)
        specs.append(_const_spec((1, d)))
    return pl.pallas_call(
        functools.partial(_ffn_kernel, tm=tm, final=final),
        grid=(b, lp // tm),
        in_specs=specs,
        out_specs=_row_spec(tm, d),
        out_shape=jax.ShapeDtypeStruct((b, lp, d), F32),
        scratch_shapes=[pltpu.VMEM((SUBLANES, D_FF), F32), pltpu.VMEM((tm, D_FF), BF16)],
        compiler_params=_params(("parallel", "arbitrary")),
        name="ffn",
    )(*ins)


def _pad_rows(m, rows, at=0):
    return jnp.zeros((rows, m.shape[1]), m.dtype).at[at:at + m.shape[0]].set(m)


def _block_diag(w):
    hn, n, _ = w.shape
    out = jnp.zeros((hn * n, hn * n), w.dtype)
    for i in range(hn):
        out = out.at[i * n:(i + 1) * n, i * n:(i + 1) * n].set(w[i])
    return out


def kernel(x, meta, w_in, rw_mu, s5_a_re, s5_a_im, s5_log_dt, s5_b_re, s5_b_im, s5_c_re, s5_c_im, s5_d, s5_w_glu, s5_b_glu, gla_w_gate, gla_b_gate, gla_norm, rw_w0, rw_w2, rw_a0, rw_a2, rw_g2, rw_k_k, rw_k_a, rw_r_k, rw_ln_w, rw_ln_b, rw_v0, rw_v1, rw_v2, lru_conv_w, lru_conv_b, lru_w_a, lru_b_a, lru_w_i, lru_b_i, lru_lam, w_branch, w_merge_gate, b_merge_gate, w_out, norm_mix, norm_ffn, ffn_w_up, ffn_conv_w, ffn_conv_b, ffn_w_down, norm_final):
    bsz, seq, d = x.shape
    assert d == D_MODEL and seq % CHUNK == 0
    lp = FRONT + seq
    tm = _pick_tile(lp, 1040, SUBLANES)
    tchunk = _pick_tile(lp, 832)
    tstream = _pick_tile(lp, STREAM_DEPTH * CHUNK)
    row = lambda v: v.astype(F32).reshape(1, -1)

    h = jnp.concatenate([jnp.zeros((bsz, PAD, d), x.dtype),
                         jnp.broadcast_to(meta.astype(x.dtype)[None], (bsz, N_META, d)), x], axis=1)
    v_first = None
    for l in range(DEPTH):
        wl = w_in[l]
        o_gla = BRANCH_W
        o_rw = o_gla + 784
        o_lru = o_rw + 896
        hq = GLA_HEADS * GLA_DK
        gla_cols = jnp.concatenate([
            wl[:, o_gla:o_gla + 2 * hq + BRANCH_W],
            wl[:, o_gla + 2 * hq + BRANCH_W + GLA_GATE_RANK:o_rw],
            wl[:, o_gla + 2 * hq + BRANCH_W:o_gla + 2 * hq + BRANCH_W + GLA_GATE_RANK],
            jnp.zeros((d, LANES - GLA_GATE_RANK), wl.dtype)], axis=1)
        rw_cols = wl[:, o_rw:o_lru]
        if l > 0:
            rw_cols = jnp.concatenate([rw_cols, rw_v1[l - 1],
                                       jnp.zeros((d, LANES - RW_V_RANK), wl.dtype)], axis=1)
        wcat = jnp.concatenate([wl[:, 0:BRANCH_W], gla_cols, rw_cols, wl[:, o_lru:]], axis=1).astype(BF16)
        widths = (BRANCH_W, GLA_W, RW_WV if l > 0 else RW_W, LRU_W)
        p5, pg, pr, pu = _proj(h, row(norm_mix[l]), wcat, widths, tm)

        y5 = _s5(p5, _s5_mats(s5_a_re[l], s5_a_im[l], s5_log_dt[l], s5_b_re[l], s5_b_im[l],
                              s5_c_re[l], s5_c_im[l]))
        yg = _gla(pg, _pad_rows(gla_w_gate[l].astype(F32), LANES), row(gla_b_gate[l]), row(gla_norm[l]), tstream)
        prm = dict(mu=row(rw_mu[l]), w0=row(rw_w0[l]), w2=_pad_rows(rw_w2[l].astype(F32), LANES, 0),
                   a0=row(rw_a0[l]), a2=_pad_rows(rw_a2[l].astype(F32), LANES, RW_W_RANK),
                   g2=_pad_rows(rw_g2[l].astype(F32), LANES, RW_W_RANK + RW_A_RANK),
                   k_k=row(rw_k_k[l]), k_a=row(rw_k_a[l]), r_k=row(rw_r_k[l]),
                   ln_w=row(rw_ln_w[l]), ln_b=row(rw_ln_b[l]))
        if l > 0:
            prm["v0"] = row(rw_v0[l - 1])
            prm["v2"] = _pad_rows(rw_v2[l - 1].astype(F32), LANES)
        yr, v_first = _rwkv(pr, v_first, prm, tstream)
        wgi = jnp.concatenate([_block_diag(lru_w_a[l]), _block_diag(lru_w_i[l])], axis=1).astype(BF16)
        bgi = jnp.concatenate([lru_b_a[l], lru_b_i[l]]).astype(F32).reshape(1, -1)
        yl = _lru(pu, lru_conv_w[l].astype(F32), row(lru_conv_b[l]), wgi, bgi, row(lru_lam[l]), tchunk)

        h = _merge(h, row(norm_mix[l]), p5, y5, yg, yr, yl, row(s5_d[l]), s5_w_glu[l].astype(BF16),
                   row(s5_b_glu[l]), w_merge_gate[l].astype(BF16),
                   b_merge_gate[l].astype(F32).reshape(4, 1, d), w_branch[l].astype(BF16),
                   w_out[l].astype(BF16), tm)
        h = _ffn(h, row(norm_ffn[l]), ffn_w_up[l].astype(BF16), ffn_conv_w[l].astype(F32),
                 row(ffn_conv_b[l]), ffn_w_down[l].astype(BF16),
                 row(norm_final) if l == DEPTH - 1 else None, tm)
    return h[:, FRONT:]
```

```python
import functools
import math

import numpy as np
import jax
import jax.numpy as jnp
from jax import lax
from jax.experimental import pallas as pl
from jax.experimental.pallas import tpu as pltpu

F32 = jnp.float32
BF16 = jnp.bfloat16

D_MODEL = 1024
DEPTH = 4
N_META = 16
BRANCH_W = 256
S5_GROUP = 16
S5_GROUPS = 16
S5_STATE = 64
S5_BLOCK = 16
GLA_HEADS = 4
GLA_DK = 32
GLA_DV = 64
GLA_GATE_RANK = 16
GLA_TAU = 16.0
CHUNK = 64
RW_HEADS = 4
RW_HEAD = 64
RW_W_RANK = 32
RW_A_RANK = 32
RW_V_RANK = 16
RW_G_RANK = 64
RW_LN_EPS = 64e-5
LRU_C = 8.0
LRU_CONV_W = 4
D_FF = 2816
FFN_CONV_W = 3
RMS_EPS = 1e-6

PAD = (-N_META) % CHUNK
FRONT = PAD + N_META
LANES = 128
SUBLANES = 8
VMEM_LIMIT = 56 * 1024 * 1024

GLA_W = 896
RW_W = 896
RW_WV = 1024
LRU_W = 512


def _mm(a, b):
    return jnp.dot(a.astype(BF16), b.astype(BF16), preferred_element_type=F32)


def _mm_nt(a, b):
    return lax.dot_general(a.astype(BF16), b.astype(BF16), (((1,), (1,)), ((), ())),
                           preferred_element_type=F32)


def _mm_tn(a, b):
    return lax.dot_general(a.astype(BF16), b.astype(BF16), (((0,), (0,)), ((), ())),
                           preferred_element_type=F32)


def _split2(a):
    hi = a.astype(BF16)
    lo = (a - hi.astype(F32)).astype(BF16)
    return hi, lo


def _split3(a):
    hi = a.astype(BF16)
    r1 = a - hi.astype(F32)
    mid = r1.astype(BF16)
    lo = (r1 - mid.astype(F32)).astype(BF16)
    return hi, mid, lo


def _dg(a, b, dims):
    return lax.dot_general(a, b, (dims, ((), ())), preferred_element_type=F32)


def _mm3_dims(a, b, dims):
    ah, al = _split2(a)
    bh, bl = _split2(b)
    return _dg(ah, bh, dims) + (_dg(ah, bl, dims) + _dg(al, bh, dims))


_NN = ((1,), (0,))
_NT = ((1,), (1,))
_TN = ((0,), (0,))


def _mm3(a, b):
    return _mm3_dims(a, b, _NN)


def _mm3_nt(a, b):
    return _mm3_dims(a, b, _NT)


def _mm3_tn(a, b):
    return _mm3_dims(a, b, _TN)


def _mm_exact_rhs(a, b_bf16):
    hi, mid, lo = _split3(a)
    return _dg(hi, b_bf16, _NN) + (_dg(mid, b_bf16, _NN) + _dg(lo, b_bf16, _NN))


def _mm_exact_lhs(a_bf16, b):
    hi, mid, lo = _split3(b)
    return _dg(a_bf16, hi, _NN) + (_dg(a_bf16, mid, _NN) + _dg(a_bf16, lo, _NN))


def _softplus(x):
    return jnp.maximum(x, 0.0) + jnp.log1p(jnp.exp(-jnp.abs(x)))


def _sigmoid(x):
    return jax.nn.sigmoid(x)


def _silu(x):
    return x * jax.nn.sigmoid(x)


def _gelu(x):
    return jax.nn.gelu(x)


def _rmsnorm_rows(x, g):
    ms = jnp.mean(x * x, axis=-1, keepdims=True)
    return x * lax.rsqrt(ms + RMS_EPS) * g


def _iota(shape, dim):
    return lax.broadcasted_iota(jnp.int32, shape, dim)


def _div(x, n):
    return x >> int(math.log2(n))


def _mod(x, n):
    return x & (n - 1)


def _expand_heads(x, block_mask):
    return jnp.where(block_mask, _tile_rows4(x), 0.0)


def _tile_rows4(x):
    return jnp.concatenate([x, x, x, x], axis=0)


def _fold_rows4(z):
    n = z.shape[0] // 4
    return (z[0:n] + z[n:2 * n]) + (z[2 * n:3 * n] + z[3 * n:4 * n])


def _shift_rows(x, s, carry8):
    xs = pltpu.roll(x, s, axis=0)
    top = jnp.where(_iota((SUBLANES, x.shape[1]), 0) < s,
                    pltpu.roll(carry8, s, axis=0), xs[0:SUBLANES])
    return jnp.concatenate([top, xs[SUBLANES:]], axis=0)


def _lockstep(gens, skew):
    gens = list(gens)
    slot = 0
    live = list(range(len(gens)))
    while live:
        nxt = []
        for i in live:
            if slot < i * skew:
                nxt.append(i)
                continue
            try:
                next(gens[i])
                nxt.append(i)
            except StopIteration:
                pass
        live = nxt
        slot += 1


def _loop_streams(n, rows, chunk_gen, depth, skew):
    def trip(i, carry):
        _lockstep([chunk_gen(r, depth * i + d) for d in range(depth) for r in range(rows)], skew)
        return carry

    if n >= depth:
        lax.fori_loop(0, n // depth, trip, 0)
    tail = n % depth
    if tail:
        _lockstep([chunk_gen(r, n - tail + d) for d in range(tail) for r in range(rows)], skew)


def _chunk_rows(c):
    start = c * CHUNK
    return pl.ds(start if isinstance(c, int) else pl.multiple_of(start, CHUNK), CHUNK)


def _const_spec(shape):
    nd = len(shape)
    return pl.BlockSpec(shape, lambda *_: (0,) * nd, pipeline_mode=pl.Buffered(1))


def _row_spec(rows, width, nb=1):
    return pl.BlockSpec((nb, rows, width), lambda b, j: (b, j, 0))


STREAM_ROWS = 2
STREAM_DEPTH = 5
RW_SKEW = 1
GLA_SKEW = 0


def _stream_rows(b):
    return STREAM_ROWS if b % STREAM_ROWS == 0 else 1


def _params(sem):
    return pltpu.CompilerParams(dimension_semantics=sem, vmem_limit_bytes=VMEM_LIMIT)


def _pick_tile(lp, cap, step=CHUNK):
    best = step
    t = step
    while t <= min(lp, cap):
        if lp % t == 0:
            best = t
        t += step
    return best


def _proj_kernel(h_ref, g_ref, w_ref, *out_refs, widths):
    xn = _rmsnorm_rows(h_ref[0], g_ref[...]).astype(BF16)
    off = 0
    for o_ref, w in zip(out_refs, widths):
        o_ref[0] = jnp.dot(xn, w_ref[:, off:off + w], preferred_element_type=F32)
        off += w


def _proj(h, g, w, widths, tm):
    b, lp, d = h.shape
    nt = lp // tm
    return pl.pallas_call(
        functools.partial(_proj_kernel, widths=widths),
        grid=(b, nt),
        in_specs=[_row_spec(tm, d), _const_spec((1, d)), _const_spec(w.shape)],
        out_specs=[_row_spec(tm, wd) for wd in widths],
        out_shape=[jax.ShapeDtypeStruct((b, lp, wd), F32) for wd in widths],
        compiler_params=_params(("parallel", "parallel")),
        name="proj",
    )(h, g, w)


def _cmul(pr, pi, zr, zi):
    return pr * zr - pi * zi, pr * zi + pi * zr


def _s5_kernel(u_ref, kd_ref, bb_ref, cc_ref, pe_ref, pc_ref, a_ref, y_ref,
               io_scr, xs_scr, ys_scr, e_scr, sp_scr, *, nb, nbp):
    ns = S5_GROUPS * S5_STATE
    pad = nbp - nb
    halves = BRANCH_W // LANES
    for hf in range(halves):
        io_scr[hf] = u_ref[0, :, hf * LANES:(hf + 1) * LANES]
    for s in range(S5_BLOCK):
        for hf in range(halves):
            xs_scr[s, 0:nb, hf * LANES:(hf + 1) * LANES] = (
                io_scr[hf, pl.ds(s, nb, stride=S5_BLOCK), :].astype(BF16))
        if pad:
            xs_scr[s, nb:nbp, :] = jnp.zeros((pad, BRANCH_W), BF16)
    if pad:
        sp_scr[nb:nbp, :] = jnp.zeros((pad, 2 * ns), F32)

    e_re = None
    for s in range(S5_BLOCK):
        z = jnp.dot(xs_scr[s], bb_ref[...], preferred_element_type=F32)
        zr, zi = _cmul(pe_ref[s:s + 1, 0:ns], pe_ref[s:s + 1, ns:], z[:, 0:ns], z[:, ns:])
        e_re, e_im = (zr, zi) if e_re is None else (e_re + zr, e_im + zi)
    e_scr[:, 0:ns] = e_re
    e_scr[:, ns:] = e_im

    ar = a_ref[0:1, 0:ns]
    ai = a_ref[0:1, ns:]

    def step(k, carry):
        sr, si = carry
        sp_scr[pl.ds(k, 1), 0:ns] = sr
        sp_scr[pl.ds(k, 1), ns:] = si
        nr, ni = _cmul(ar, ai, sr, si)
        return nr + e_scr[pl.ds(k, 1), 0:ns], ni + e_scr[pl.ds(k, 1), ns:]

    zero = jnp.zeros((1, ns), F32)
    lax.fori_loop(0, nb, step, (zero, zero))

    for d in range(S5_BLOCK):
        m = S5_BLOCK - d
        prod = jnp.dot(xs_scr[0:m].reshape(m * nbp, BRANCH_W), kd_ref[d], preferred_element_type=F32)
        prod = prod.reshape(m, nbp, BRANCH_W)
        if d == 0:
            ys_scr[...] = prod
        else:
            ys_scr[d:S5_BLOCK] = ys_scr[d:S5_BLOCK] + prod

    spr = sp_scr[:, 0:ns]
    spi = sp_scr[:, ns:]
    for t in range(S5_BLOCK):
        gr, gi = _cmul(pc_ref[t:t + 1, 0:ns], pc_ref[t:t + 1, ns:], spr, spi)
        g = jnp.concatenate([gr, gi], axis=1).astype(BF16)
        yt = ys_scr[t] + jnp.dot(g, cc_ref[...], preferred_element_type=F32)
        for hf in range(halves):
            io_scr[hf, pl.ds(t, nb, stride=S5_BLOCK), :] = yt[0:nb, hf * LANES:(hf + 1) * LANES]
    for hf in range(halves):
        y_ref[0, :, hf * LANES:(hf + 1) * LANES] = io_scr[hf]


def _s5_mats(a_re, a_im, log_dt, b_re, b_im, c_re, c_im):
    lr, li = a_re.astype(F32), a_im.astype(F32)
    dt = jnp.exp(log_dt.astype(F32))[:, None]
    mag = jnp.exp(lr * dt)
    abr, abi = mag * jnp.cos(li * dt), mag * jnp.sin(li * dt)
    den = lr * lr + li * li
    cr = ((abr - 1.0) * lr + abi * li) / den
    ci = (abi * lr - (abr - 1.0) * li) / den
    d = jnp.arange(S5_BLOCK + 1, dtype=F32)[:, None, None]
    pmag = jnp.exp(lr * dt * d)
    p_re, p_im = pmag * jnp.cos(li * dt * d), pmag * jnp.sin(li * dt * d)
    bb_re = cr[..., None] * b_re - ci[..., None] * b_im
    bb_im = cr[..., None] * b_im + ci[..., None] * b_re
    cp_re = c_re[None] * p_re[:, :, None, :] - c_im[None] * p_im[:, :, None, :]
    cp_im = c_re[None] * p_im[:, :, None, :] + c_im[None] * p_re[:, :, None, :]
    hp = lax.Precision.HIGHEST
    kd = (jnp.einsum("dgon,gni->dgoi", cp_re, bb_re, precision=hp)
          - jnp.einsum("dgon,gni->dgoi", cp_im, bb_im, precision=hp))
    eye_g = jnp.eye(S5_GROUPS, dtype=F32)
    kmat = jnp.einsum("dgoi,gq->dgiqo", kd[:S5_BLOCK], eye_g).reshape(S5_BLOCK, BRANCH_W, BRANCH_W)
    ns = S5_GROUPS * S5_STATE
    bmat = jnp.concatenate([jnp.einsum("gni,gq->giqn", bb_re, eye_g).reshape(BRANCH_W, ns),
                            jnp.einsum("gni,gq->giqn", bb_im, eye_g).reshape(BRANCH_W, ns)], axis=1)
    cmat = jnp.concatenate([jnp.einsum("gon,gq->gnqo", c_re.astype(F32), eye_g).reshape(ns, BRANCH_W),
                            jnp.einsum("gon,gq->gnqo", -c_im.astype(F32), eye_g).reshape(ns, BRANCH_W)], axis=0)
    flat = lambda z: z.reshape(z.shape[0], ns)
    t = np.arange(S5_BLOCK)
    pe = jnp.concatenate([flat(p_re[S5_BLOCK - 1 - t]), flat(p_im[S5_BLOCK - 1 - t])], axis=1)
    pc = jnp.concatenate([flat(p_re[1:]), flat(p_im[1:])], axis=1)
    a16 = jnp.concatenate([flat(p_re[S5_BLOCK:]), flat(p_im[S5_BLOCK:])], axis=1)
    a16 = jnp.pad(a16, ((0, SUBLANES - 1), (0, 0)))
    return kmat.astype(BF16), bmat.astype(BF16), cmat.astype(BF16), pe, pc, a16


def _s5(p5, mats):
    kmat, bmat, cmat, pe, pc, a16 = mats
    b, lp, _ = p5.shape
    nb = lp // S5_BLOCK
    nbp = -(-nb // 16) * 16
    ns2 = 2 * S5_GROUPS * S5_STATE
    return pl.pallas_call(
        functools.partial(_s5_kernel, nb=nb, nbp=nbp),
        grid=(b,),
        in_specs=[pl.BlockSpec((1, lp, BRANCH_W), lambda i: (i, 0, 0)), _const_spec(kmat.shape),
                  _const_spec(bmat.shape), _const_spec(cmat.shape), _const_spec(pe.shape),
                  _const_spec(pc.shape), _const_spec(a16.shape)],
        out_specs=pl.BlockSpec((1, lp, BRANCH_W), lambda i: (i, 0, 0)),
        out_shape=jax.ShapeDtypeStruct((b, lp, BRANCH_W), F32),
        scratch_shapes=[pltpu.VMEM((BRANCH_W // LANES, lp, LANES), F32),
                        pltpu.VMEM((S5_BLOCK, nbp, BRANCH_W), BF16),
                        pltpu.VMEM((S5_BLOCK, nbp, BRANCH_W), F32),
                        pltpu.VMEM((nbp, ns2), F32),
                        pltpu.VMEM((nbp, ns2), F32)],
        compiler_params=_params(("parallel",)),
        name="s5",
    )(p5, kmat, bmat, cmat, pe, pc, a16)


def _gla_kernel(p_ref, wg_ref, bg_ref, nw_ref, y_ref, st_scr, *, cpt, nrows):
    @pl.when(pl.program_id(1) == 0)
    def _():
        st_scr[...] = jnp.zeros_like(st_scr)

    hq = GLA_HEADS * GLA_DK
    hv = GLA_HEADS * GLA_DV
    rows4 = GLA_HEADS * CHUNK
    tri = (_iota((CHUNK, CHUNK), 0) >= _iota((CHUNK, CHUNK), 1)).astype(BF16)
    bd_k = _div(_iota((rows4, hq), 0), CHUNK) == _div(_iota((rows4, hq), 1), GLA_DK)
    causal = _mod(_iota((CHUNK, rows4), 1), CHUNK) <= _iota((CHUNK, rows4), 0)
    bd_v = _div(_iota((rows4, hv), 0), CHUNK) == _div(_iota((rows4, hv), 1), GLA_DV)
    bd_st = _div(_iota((hv, hq), 0), GLA_DV) == _div(_iota((hv, hq), 1), GLA_DK)
    head_ones = (_div(_iota((hv, hv), 0), GLA_DV) == _div(_iota((hv, hv), 1), GLA_DV)).astype(BF16)
    wg = wg_ref[...]
    bg = bg_ref[...]
    nw = nw_ref[...]

    def chunk(bi, c):
        rows = _chunk_rows(c)
        p = p_ref[bi, rows, :]
        q = p[:, 0:hq]
        k = p[:, hq:2 * hq]
        v = p[:, 2 * hq:2 * hq + hv]
        og = p[:, 2 * hq + hv:2 * hq + 2 * hv]
        gd = p[:, 2 * hq + 2 * hv:]
        x = _mm3(gd, wg) + bg
        yield
        g = (jnp.minimum(x, 0.0) - jnp.log1p(jnp.exp(-jnp.abs(x)))) / GLA_TAU
        gc = _mm_exact_lhs(tri, g)
        yield
        gl = gc[CHUNK - 1:CHUNK, :]
        qg = q * (GLA_DK ** -0.5) * jnp.exp(gc)
        kg = k * jnp.exp(-gc)
        kl = k * jnp.exp(gl - gc)
        kg_e = _expand_heads(kg, bd_k)
        att = _mm_nt(qg, kg_e)
        kv = _mm_tn(v, kl)
        yield
        att = jnp.where(causal, att, 0.0)
        v_e = _expand_heads(v, bd_v)
        st = st_scr[bi]
        o = _mm(att, v_e) + _mm_nt(qg, st)
        st_scr[bi] = st * jnp.exp(gl) + jnp.where(bd_st, kv, 0.0)
        yield
        ms = _mm(o * o, head_ones) * (1.0 / GLA_DV)
        yield
        o = o * lax.rsqrt(ms + RMS_EPS) * nw
        y_ref[bi, rows, :] = o * _silu(og)

    _loop_streams(cpt, nrows, chunk, STREAM_DEPTH, GLA_SKEW)


def _gla(pg, w_gate_pad, b_gate, norm_w, tile):
    b, lp, _ = pg.shape
    cpt = tile // CHUNK
    nrows = _stream_rows(b)
    return pl.pallas_call(
        functools.partial(_gla_kernel, cpt=cpt, nrows=nrows),
        grid=(b // nrows, lp // tile),
        in_specs=[_row_spec(tile, GLA_W, nrows), _const_spec(w_gate_pad.shape),
                  _const_spec(b_gate.shape), _const_spec(norm_w.shape)],
        out_specs=_row_spec(tile, BRANCH_W, nrows),
        out_shape=jax.ShapeDtypeStruct((b, lp, BRANCH_W), F32),
        scratch_shapes=[pltpu.VMEM((nrows, GLA_HEADS * GLA_DV, GLA_HEADS * GLA_DK), F32)],
        compiler_params=_params(("parallel", "arbitrary")),
        name="gla",
    )(pg, w_gate_pad, b_gate, norm_w)


def _rw_kernel(*refs, cpt, nrows, has_vgate):
    if has_vgate:
        (p_ref, vf_ref, mu_ref, w0_ref, w2_ref, a0_ref, a2_ref, g2_ref, kk_ref, ka_ref, rk_ref,
         lnw_ref, lnb_ref, v0_ref, v2_ref, y_ref, carry_scr, h_scr) = refs
    else:
        (p_ref, mu_ref, w0_ref, w2_ref, a0_ref, a2_ref, g2_ref, kk_ref, ka_ref, rk_ref,
         lnw_ref, lnb_ref, y_ref, vf_out_ref, carry_scr, h_scr) = refs

    @pl.when(pl.program_id(1) == 0)
    def _():
        carry_scr[...] = jnp.zeros_like(carry_scr)
        h_scr[...] = jnp.zeros_like(h_scr)

    w = BRANCH_W
    rows4 = RW_HEADS * CHUNK
    r_i = _iota((rows4, w), 0)
    c_i = _iota((rows4, w), 1)
    bd = _div(r_i, CHUNK) == _div(c_i, RW_HEAD)
    eye = (r_i == c_i).astype(F32)
    t_row = _iota((CHUNK, w), 0)
    s_col = _mod(_iota((CHUNK, w), 1), CHUNK)
    strict = s_col < t_row
    incl = s_col <= t_row
    eye_c = (s_col == t_row).astype(F32)
    tri = (_iota((CHUNK, CHUNK), 0) >= _iota((CHUNK, CHUNK), 1)).astype(BF16)
    head_ones = bd.astype(BF16)
    row0 = _iota((CHUNK, RW_W), 0) == 0
    zeros_cw = jnp.zeros((CHUNK, w), F32)
    mu = mu_ref[...]
    inv_n = 1.0 / RW_HEAD

    def expand(x):
        return _expand_heads(x, bd)

    def headsum(x):
        return _mm(x, head_ones)

    def chunk(bi, c):
        rows = _chunk_rows(c)
        p = p_ref[bi, rows, :]
        pf = p[:, 0:RW_W]
        prev = jnp.where(row0, carry_scr[bi, 0:1, :], pltpu.roll(pf, 1, axis=0))
        carry_scr[bi, 0:1, :] = pf[CHUNK - 1:CHUNK, :]
        xs = pf + mu * (prev - pf)
        r = xs[:, 0:w]
        k = xs[:, w:2 * w]
        v = xs[:, 2 * w:3 * w]
        lr = xs[:, 3 * w:]
        wlow = _mm3(jnp.tanh(lr), w2_ref[...])
        alow = _mm(lr, a2_ref[...])
        g = _mm(_sigmoid(lr), g2_ref[...])
        if has_vgate:
            vlow = _mm(p[:, RW_W:], v2_ref[...])
        kk = k * kk_ref[...]
        kss = headsum(kk * kk)
        yield
        w_log = -_softplus(-(w0_ref[...] + wlow)) - 0.5
        lw = -jnp.exp(w_log)
        cum = _mm_exact_lhs(tri, lw)
        ag = _sigmoid(a0_ref[...] + alow)
        if has_vgate:
            v = v + (vf_ref[bi, rows, :] - v) * _sigmoid(v0_ref[...] + vlow)
        else:
            vf_out_ref[bi, rows, :] = v
        kk = kk * lax.rsqrt(jnp.maximum(kss, 1e-24))
        k2 = k * (1.0 + (ag - 1.0) * ka_ref[...])
        bvec = kk * ag
        rk_sum = headsum(r * k2 * rk_ref[...])
        v_e = expand(v)
        yield
        tot = cum[CHUNK - 1:CHUNK, :]
        e_neg = jnp.exp(-cum)
        e_tail = jnp.exp(tot - cum)
        a_n = -kk * jnp.exp(cum - lw)
        r_n = r * jnp.exp(cum)
        lhs = jnp.concatenate([a_n, r_n], axis=0)
        rhs = jnp.concatenate([expand(bvec * e_neg), expand(k2 * e_neg)], axis=0)
        aa = _mm_nt(lhs, rhs)
        yield
        a_ab = jnp.where(strict, aa[0:CHUNK, 0:w], 0.0)
        a_ak = jnp.where(strict, aa[0:CHUNK, w:], 0.0)
        a_rb = jnp.where(incl, aa[CHUNK:, 0:w], 0.0)
        a_rk = jnp.where(incl, aa[CHUNK:, w:], 0.0)
        tinv = eye_c + a_ab
        pw = a_ab
        pw_e = expand(pw)
        akv = _mm(a_ak, v_e)
        rkv = _mm(a_rk, v_e)
        for _ in range(5):
            pw = _mm(pw, pw_e)
            yield
            pw_e = expand(pw)
            tinv = tinv + _mm(tinv, pw_e)
        yield
        x_n = _mm(tinv, jnp.concatenate([expand(a_n), expand(akv)], axis=1))
        yield
        yq = _mm(a_rb, jnp.concatenate([expand(x_n[:, 0:w]), expand(x_n[:, w:])], axis=1))
        mn = _mm_tn(jnp.concatenate([bvec * e_tail, k2 * e_tail], axis=0),
                    jnp.concatenate([x_n, jnp.concatenate([zeros_cw, v], axis=1)], axis=0))
        yield
        q_n = r_n + yq[:, 0:w]
        ol_n = yq[:, w:] + rkv
        m_mat = jnp.where(bd, mn[:, 0:w], 0.0) + eye * jnp.exp(tot)
        n_mat = jnp.where(bd, mn[:, w:], 0.0)
        h = h_scr[bi]
        o = ol_n + _mm(q_n, h)
        h_scr[bi] = _mm(m_mat, h) + n_mat
        yield
        mean = headsum(o) * inv_n
        yield
        dlt = o - mean
        var = headsum(dlt * dlt) * inv_n
        yield
        o = dlt * lax.rsqrt(var + RW_LN_EPS) * lnw_ref[...] + lnb_ref[...]
        y_ref[bi, rows, :] = (o + rk_sum * v) * g

    _loop_streams(cpt, nrows, chunk, STREAM_DEPTH, RW_SKEW)


def _rwkv(pr, v_first, prm, tile):
    b, lp, pw = pr.shape
    has_vgate = v_first is not None
    cpt = tile // CHUNK
    vec = _const_spec((1, BRANCH_W))
    mat = _const_spec((LANES, BRANCH_W))
    nrows = _stream_rows(b)
    ins = [pr]
    specs = [_row_spec(tile, pw, nrows)]
    if has_vgate:
        ins.append(v_first)
        specs.append(_row_spec(tile, BRANCH_W, nrows))
    ins += [prm["mu"], prm["w0"], prm["w2"], prm["a0"], prm["a2"], prm["g2"], prm["k_k"], prm["k_a"],
            prm["r_k"], prm["ln_w"], prm["ln_b"]]
    specs += [_const_spec((1, RW_W)), vec, mat, vec, mat, mat, vec, vec, vec, vec, vec]
    out_shape = [jax.ShapeDtypeStruct((b, lp, BRANCH_W), F32)]
    out_specs = [_row_spec(tile, BRANCH_W, nrows)]
    if has_vgate:
        ins += [prm["v0"], prm["v2"]]
        specs += [vec, mat]
    else:
        out_shape.append(jax.ShapeDtypeStruct((b, lp, BRANCH_W), F32))
        out_specs.append(_row_spec(tile, BRANCH_W, nrows))
    outs = pl.pallas_call(
        functools.partial(_rw_kernel, cpt=cpt, nrows=nrows, has_vgate=has_vgate),
        grid=(b // nrows, lp // tile),
        in_specs=specs,
        out_specs=out_specs,
        out_shape=out_shape,
        scratch_shapes=[pltpu.VMEM((nrows, SUBLANES, RW_W), F32),
                        pltpu.VMEM((nrows, RW_HEADS * RW_HEAD, BRANCH_W), F32)],
        compiler_params=_params(("parallel", "arbitrary")),
        name="rwkv7",
    )(*ins)
    if has_vgate:
        return outs[0], v_first
    return outs[0], outs[1]


def _lru_kernel(p_ref, cw_ref, cb_ref, wg_ref, bg_ref, lam_ref, y_ref, x_scr, h_scr, a_scr, b_scr, *, tl):
    j = pl.program_id(1)

    @pl.when(j == 0)
    def _():
        x_scr[...] = jnp.zeros_like(x_scr)
        h_scr[...] = jnp.zeros_like(h_scr)

    w = BRANCH_W
    p = p_ref[0]
    xr = p[:, 0:w]
    gate = p[:, w:]
    carry8 = x_scr[...]
    xc = cb_ref[...] + cw_ref[LRU_CONV_W - 1:LRU_CONV_W, :] * xr
    for s in range(1, LRU_CONV_W):
        xc = xc + cw_ref[LRU_CONV_W - 1 - s:LRU_CONV_W - s, :] * _shift_rows(xr, s, carry8)
    x_scr[...] = xr[tl - SUBLANES:tl, :]
    gates = _mm(xc, wg_ref[...]) + bg_ref[...]
    rg = _sigmoid(gates[:, 0:w])
    ig = _sigmoid(gates[:, w:])
    log_a = -LRU_C * rg * _softplus(-lam_ref[...])
    a = jnp.exp(log_a)
    mult = jnp.sqrt(1.0 - jnp.exp(2.0 * log_a))
    ridx = j * tl + _iota((tl, w), 0)
    mult = jnp.where(ridx == PAD, 1.0, mult)
    bterm = jnp.where(ridx >= PAD, mult * ig * xc, 0.0)

    ng = tl // SUBLANES
    g_i = _iota((ng, LANES), 0)
    for hf in range(w // LANES):
        lanes = slice(hf * LANES, (hf + 1) * LANES)
        a_scr[hf] = a[:, lanes]
        b_scr[hf] = bterm[:, lanes]
        prods, sums = [], []
        for r in range(SUBLANES):
            a_r = a_scr[hf, pl.ds(r, ng, stride=SUBLANES), :]
            b_r = b_scr[hf, pl.ds(r, ng, stride=SUBLANES), :]
            prods.append(a_r if r == 0 else a_r * prods[-1])
            sums.append(b_r if r == 0 else a_r * sums[-1] + b_r)
        pt, ct = prods[-1], sums[-1]
        d = 1
        while d < ng:
            keep = g_i >= d
            ct = ct + pt * jnp.where(keep, pltpu.roll(ct, d, axis=0), 0.0)
            pt = pt * jnp.where(keep, pltpu.roll(pt, d, axis=0), 1.0)
            d *= 2
        h_end = ct + pt * h_scr[0:1, lanes]
        h_in = jnp.where(g_i == 0, h_scr[0:1, lanes], pltpu.roll(h_end, 1, axis=0))
        h_scr[0:1, lanes] = h_end[ng - 1:ng, :]
        for r in range(SUBLANES):
            b_scr[hf, pl.ds(r, ng, stride=SUBLANES), :] = sums[r] + prods[r] * h_in
        y_ref[0, :, lanes] = b_scr[hf] * _gelu(gate[:, lanes])


def _lru(pl_, cw, cb, wg, bg, lam, tile):
    b, lp, _ = pl_.shape
    vec = _const_spec((1, BRANCH_W))
    return pl.pallas_call(
        functools.partial(_lru_kernel, tl=tile),
        grid=(b, lp // tile),
        in_specs=[_row_spec(tile, LRU_W), _const_spec(cw.shape), vec, _const_spec(wg.shape),
                  _const_spec(bg.shape), vec],
        out_specs=_row_spec(tile, BRANCH_W),
        out_shape=jax.ShapeDtypeStruct((b, lp, BRANCH_W), F32),
        scratch_shapes=[pltpu.VMEM((SUBLANES, BRANCH_W), F32), pltpu.VMEM((SUBLANES, BRANCH_W), F32),
                        pltpu.VMEM((BRANCH_W // LANES, tile, LANES), F32),
                        pltpu.VMEM((BRANCH_W // LANES, tile, LANES), F32)],
        compiler_params=_params(("parallel", "arbitrary")),
        name="rglru",
    )(pl_, cw, cb, wg, bg, lam)


MERGE_COLS = 256


def _merge_kernel(h_ref, g_ref, u5_ref, y5_ref, yg_ref, yr_ref, yl_ref, d5_ref, wglu_ref, bglu_ref,
                  wgate_ref, bgate_ref, wbr_ref, wout_ref, o_ref, m_scr, *, tm):
    x = h_ref[0]
    xn = _rmsnorm_rows(x, g_ref[...]).astype(BF16)
    ys = _gelu(y5_ref[0] + d5_ref[...] * u5_ref[0])
    ys = ys * _sigmoid(_mm(ys, wglu_ref[...]) + bglu_ref[...])
    branches = tuple(yb.astype(BF16) for yb in (ys, yg_ref[0], yr_ref[0], yl_ref[0]))
    for c in range(D_MODEL // MERGE_COLS):
        cols = slice(c * MERGE_COLS, (c + 1) * MERGE_COLS)
        merged = None
        for i, yb in enumerate(branches):
            gate = _sigmoid(jnp.dot(xn, wgate_ref[i, :, cols], preferred_element_type=F32)
                            + bgate_ref[i, :, cols])
            term = gate * jnp.dot(yb, wbr_ref[i, :, cols], preferred_element_type=F32)
            merged = term if merged is None else merged + term
        m_scr[:, cols] = merged.astype(BF16)
    upd = jnp.dot(m_scr[...], wout_ref[...], preferred_element_type=F32)
    ridx = pl.program_id(1) * tm + _iota((tm, 1), 0)
    o_ref[0] = x + jnp.where(ridx >= PAD, upd, 0.0)


def _merge(h, g, u5, y5, yg, yr, yl, d5, wglu, bglu, wgate, bgate, wbr, wout, tm):
    b, lp, d = h.shape
    ybs = _row_spec(tm, BRANCH_W)
    return pl.pallas_call(
        functools.partial(_merge_kernel, tm=tm),
        grid=(b, lp // tm),
        in_specs=[_row_spec(tm, d), _const_spec((1, d)), ybs, ybs, ybs, ybs, ybs,
                  _const_spec((1, BRANCH_W)), _const_spec(wglu.shape), _const_spec((1, BRANCH_W)),
                  _const_spec(wgate.shape), _const_spec(bgate.shape), _const_spec(wbr.shape),
                  _const_spec(wout.shape)],
        out_specs=_row_spec(tm, d),
        out_shape=jax.ShapeDtypeStruct((b, lp, d), F32),
        scratch_shapes=[pltpu.VMEM((tm, d), BF16)],
        compiler_params=_params(("parallel", "parallel")),
        name="merge",
    )(h, g, u5, y5, yg, yr, yl, d5, wglu, bglu, wgate, bgate, wbr, wout)


FFN_COLS = 256


def _ffn_kernel(*refs, tm, final):
    if final:
        h_ref, g_ref, wup_ref, cw_ref, cb_ref, wdn_ref, gf_ref, o_ref, g_scr, act_scr = refs
    else:
        h_ref, g_ref, wup_ref, cw_ref, cb_ref, wdn_ref, o_ref, g_scr, act_scr = refs
    j = pl.program_id(1)

    @pl.when(j == 0)
    def _():
        g_scr[...] = jnp.zeros_like(g_scr)

    x = h_ref[0]
    xn = _rmsnorm_rows(x, g_ref[...]).astype(BF16)
    for c in range(D_FF // FFN_COLS):
        lo = c * FFN_COLS
        gate = jnp.dot(xn, wup_ref[:, lo:lo + FFN_COLS], preferred_element_type=F32)
        up = jnp.dot(xn, wup_ref[:, D_FF + lo:D_FF + lo + FFN_COLS], preferred_element_type=F32)
        carry8 = g_scr[:, lo:lo + FFN_COLS]
        g_scr[:, lo:lo + FFN_COLS] = gate[tm - SUBLANES:tm, :]
        gc = cb_ref[:, lo:lo + FFN_COLS] + cw_ref[FFN_CONV_W - 1:FFN_CONV_W, lo:lo + FFN_COLS] * gate
        for s in range(1, FFN_CONV_W):
            gc = gc + (cw_ref[FFN_CONV_W - 1 - s:FFN_CONV_W - s, lo:lo + FFN_COLS]
                       * _shift_rows(gate, s, carry8))
        act_scr[:, lo:lo + FFN_COLS] = (_silu(gc) * up).astype(BF16)
    acc = jnp.dot(act_scr[...], wdn_ref[...], preferred_element_type=F32)
    ridx = j * tm + _iota((tm, 1), 0)
    out = x + jnp.where(ridx >= PAD, acc, 0.0)
    if final:
        out = _rmsnorm_rows(out, gf_ref[...])
    o_ref[0] = out


def _ffn(h, g, wup, cw, cb, wdn, gf, tm):
    b, lp, d = h.shape
    final = gf is not None
    ins = [h, g, wup, cw, cb, wdn]
    specs = [_row_spec(tm, d), _const_spec((1, d)), _const_spec(wup.shape), _const_spec(cw.shape),
             _const_spec(cb.shape), _const_spec(wdn.shape)]
    if final:
        ins.append(gf)
        specs.append(_const_spec((1, d)))
    return pl.pallas_call(
        functools.partial(_ffn_kernel, tm=tm, final=final),
        grid=(b, lp // tm),
        in_specs=specs,
        out_specs=_row_spec(tm, d),
        out_shape=jax.ShapeDtypeStruct((b, lp, d), F32),
        scratch_shapes=[pltpu.VMEM((SUBLANES, D_FF), F32), pltpu.VMEM((tm, D_FF), BF16)],
        compiler_params=_params(("parallel", "arbitrary")),
        name="ffn",
    )(*ins)


def _pad_rows(m, rows, at=0):
    return jnp.zeros((rows, m.shape[1]), m.dtype).at[at:at + m.shape[0]].set(m)


def _block_diag(w):
    hn, n, _ = w.shape
    out = jnp.zeros((hn * n, hn * n), w.dtype)
    for i in range(hn):
        out = out.at[i * n:(i + 1) * n, i * n:(i + 1) * n].set(w[i])
    return out


def kernel(x, meta, w_in, rw_mu, s5_a_re, s5_a_im, s5_log_dt, s5_b_re, s5_b_im, s5_c_re, s5_c_im, s5_d, s5_w_glu, s5_b_glu, gla_w_gate, gla_b_gate, gla_norm, rw_w0, rw_w2, rw_a0, rw_a2, rw_g2, rw_k_k, rw_k_a, rw_r_k, rw_ln_w, rw_ln_b, rw_v0, rw_v1, rw_v2, lru_conv_w, lru_conv_b, lru_w_a, lru_b_a, lru_w_i, lru_b_i, lru_lam, w_branch, w_merge_gate, b_merge_gate, w_out, norm_mix, norm_ffn, ffn_w_up, ffn_conv_w, ffn_conv_b, ffn_w_down, norm_final):
    bsz, seq, d = x.shape
    assert d == D_MODEL and seq % CHUNK == 0
    lp = FRONT + seq
    tm = _pick_tile(lp, 1040, SUBLANES)
    tchunk = _pick_tile(lp, 832)
    tstream = _pick_tile(lp, STREAM_DEPTH * CHUNK)
    row = lambda v: v.astype(F32).reshape(1, -1)

    o_gla = BRANCH_W
    o_rw = o_gla + 784
    o_lru = o_rw + 896
    o_gd = o_gla + 2 * GLA_HEADS * GLA_DK + BRANCH_W
    zcols = lambda n: jnp.zeros((DEPTH, d, n), w_in.dtype)
    v1_all = jnp.concatenate([jnp.zeros((1, d, RW_V_RANK), w_in.dtype), rw_v1.astype(w_in.dtype)], axis=0)
    wcat_all = jnp.concatenate([
        w_in[:, :, 0:o_gd],
        w_in[:, :, o_gd + GLA_GATE_RANK:o_rw],
        w_in[:, :, o_gd:o_gd + GLA_GATE_RANK], zcols(LANES - GLA_GATE_RANK),
        w_in[:, :, o_rw:o_lru], v1_all, zcols(LANES - RW_V_RANK),
        w_in[:, :, o_lru:]], axis=2).astype(BF16)
    widths = (BRANCH_W, GLA_W, RW_WV, LRU_W)
    pad_rows = lambda m, at: jnp.pad(m.astype(F32), ((0, 0), (at, LANES - at - m.shape[1]), (0, 0)))
    s5_all = jax.vmap(_s5_mats)(s5_a_re, s5_a_im, s5_log_dt, s5_b_re, s5_b_im, s5_c_re, s5_c_im)
    gla_wg_all = pad_rows(gla_w_gate, 0)
    rw_w2_all = pad_rows(rw_w2, 0)
    rw_a2_all = pad_rows(rw_a2, RW_W_RANK)
    rw_g2_all = pad_rows(rw_g2, RW_W_RANK + RW_A_RANK)
    rw_v2_all = pad_rows(rw_v2, 0)
    lru_wg_all = jnp.concatenate([jax.vmap(_block_diag)(lru_w_a), jax.vmap(_block_diag)(lru_w_i)],
                                 axis=2).astype(BF16)
    lru_bg_all = jnp.concatenate([lru_b_a, lru_b_i], axis=1).astype(F32)
    s5_wglu_all = s5_w_glu.astype(BF16)
    wgate_all = w_merge_gate.astype(BF16)
    wbr_all = w_branch.astype(BF16)
    wout_all = w_out.astype(BF16)
    wup_all = ffn_w_up.astype(BF16)
    wdn_all = ffn_w_down.astype(BF16)

    h = jnp.concatenate([jnp.zeros((bsz, PAD, d), x.dtype),
                         jnp.broadcast_to(meta.astype(x.dtype)[None], (bsz, N_META, d)), x], axis=1)
    v_first = None
    for l in range(DEPTH):
        p5, pg, pr, pu = _proj(h, row(norm_mix[l]), wcat_all[l], widths, tm)
        y5 = _s5(p5, tuple(m[l] for m in s5_all))
        yg = _gla(pg, gla_wg_all[l], row(gla_b_gate[l]), row(gla_norm[l]), tstream)
        prm = dict(mu=row(rw_mu[l]), w0=row(rw_w0[l]), w2=rw_w2_all[l], a0=row(rw_a0[l]), a2=rw_a2_all[l],
                   g2=rw_g2_all[l], k_k=row(rw_k_k[l]), k_a=row(rw_k_a[l]), r_k=row(rw_r_k[l]),
                   ln_w=row(rw_ln_w[l]), ln_b=row(rw_ln_b[l]))
        if l > 0:
            prm["v0"] = row(rw_v0[l - 1])
            prm["v2"] = rw_v2_all[l - 1]
        yr, v_first = _rwkv(pr, v_first, prm, tstream)
        yl = _lru(pu, lru_conv_w[l].astype(F32), row(lru_conv_b[l]), lru_wg_all[l], lru_bg_all[l:l + 1],
                  row(lru_lam[l]), tchunk)
        h = _merge(h, row(norm_mix[l]), p5, y5, yg, yr, yl, row(s5_d[l]), s5_wglu_all[l],
                   row(s5_b_glu[l]), wgate_all[l], b_merge_gate[l].astype(F32).reshape(4, 1, d),
                   wbr_all[l], wout_all[l], tm)
        h = _ffn(h, row(norm_ffn[l]), wup_all[l], ffn_conv_w[l].astype(F32), row(ffn_conv_b[l]),
                 wdn_all[l], row(norm_final) if l == DEPTH - 1 else None, tm)
    return h[:, FRONT:]
```

```python
import functools
import math

import numpy as np
import jax
import jax.numpy as jnp
from jax import lax
from jax.experimental import pallas as pl
from jax.experimental.pallas import tpu as pltpu

F32 = jnp.float32
BF16 = jnp.bfloat16

D_MODEL = 1024
DEPTH = 4
N_META = 16
BRANCH_W = 256
S5_GROUP = 16
S5_GROUPS = 16
S5_STATE = 64
S5_BLOCK = 16
GLA_HEADS = 4
GLA_DK = 32
GLA_DV = 64
GLA_GATE_RANK = 16
GLA_TAU = 16.0
CHUNK = 64
RW_HEADS = 4
RW_HEAD = 64
RW_W_RANK = 32
RW_A_RANK = 32
RW_V_RANK = 16
RW_G_RANK = 64
RW_LN_EPS = 64e-5
LRU_C = 8.0
LRU_CONV_W = 4
D_FF = 2816
FFN_CONV_W = 3
RMS_EPS = 1e-6

PAD = (-N_META) % CHUNK
FRONT = PAD + N_META
LANES = 128
SUBLANES = 8
VMEM_LIMIT = 56 * 1024 * 1024

GLA_W = 896
RW_W = 896
RW_WV = 1024
LRU_W = 512


def _mm(a, b):
    return jnp.dot(a.astype(BF16), b.astype(BF16), preferred_element_type=F32)


def _mm_nt(a, b):
    return lax.dot_general(a.astype(BF16), b.astype(BF16), (((1,), (1,)), ((), ())),
                           preferred_element_type=F32)


def _mm_tn(a, b):
    return lax.dot_general(a.astype(BF16), b.astype(BF16), (((0,), (0,)), ((), ())),
                           preferred_element_type=F32)


def _split2(a):
    hi = a.astype(BF16)
    lo = (a - hi.astype(F32)).astype(BF16)
    return hi, lo


def _split3(a):
    hi = a.astype(BF16)
    r1 = a - hi.astype(F32)
    mid = r1.astype(BF16)
    lo = (r1 - mid.astype(F32)).astype(BF16)
    return hi, mid, lo


def _dg(a, b, dims):
    return lax.dot_general(a, b, (dims, ((), ())), preferred_element_type=F32)


def _mm3_dims(a, b, dims):
    ah, al = _split2(a)
    bh, bl = _split2(b)
    return _dg(ah, bh, dims) + (_dg(ah, bl, dims) + _dg(al, bh, dims))


_NN = ((1,), (0,))
_NT = ((1,), (1,))
_TN = ((0,), (0,))


def _mm3(a, b):
    return _mm3_dims(a, b, _NN)


def _mm3_nt(a, b):
    return _mm3_dims(a, b, _NT)


def _mm3_tn(a, b):
    return _mm3_dims(a, b, _TN)


def _mm_exact_rhs(a, b_bf16):
    hi, mid, lo = _split3(a)
    return _dg(hi, b_bf16, _NN) + (_dg(mid, b_bf16, _NN) + _dg(lo, b_bf16, _NN))


def _mm_exact_lhs(a_bf16, b):
    hi, mid, lo = _split3(b)
    return _dg(a_bf16, hi, _NN) + (_dg(a_bf16, mid, _NN) + _dg(a_bf16, lo, _NN))


def _softplus(x):
    return jnp.maximum(x, 0.0) + jnp.log1p(jnp.exp(-jnp.abs(x)))


def _sigmoid(x):
    return jax.nn.sigmoid(x)


def _silu(x):
    return x * jax.nn.sigmoid(x)


def _gelu(x):
    return jax.nn.gelu(x)


def _rmsnorm_rows(x, g):
    ms = jnp.mean(x * x, axis=-1, keepdims=True)
    return x * lax.rsqrt(ms + RMS_EPS) * g


def _iota(shape, dim):
    return lax.broadcasted_iota(jnp.int32, shape, dim)


def _div(x, n):
    return x >> int(math.log2(n))


def _mod(x, n):
    return x & (n - 1)


def _expand_heads(x, block_mask):
    return jnp.where(block_mask, _tile_rows4(x), 0.0)


def _tile_rows4(x):
    return jnp.concatenate([x, x, x, x], axis=0)


def _fold_rows4(z):
    n = z.shape[0] // 4
    return (z[0:n] + z[n:2 * n]) + (z[2 * n:3 * n] + z[3 * n:4 * n])


def _shift_rows(x, s, carry8):
    xs = pltpu.roll(x, s, axis=0)
    top = jnp.where(_iota((SUBLANES, x.shape[1]), 0) < s,
                    pltpu.roll(carry8, s, axis=0), xs[0:SUBLANES])
    return jnp.concatenate([top, xs[SUBLANES:]], axis=0)


def _lockstep(gens, skew):
    gens = list(gens)
    slot = 0
    live = list(range(len(gens)))
    while live:
        nxt = []
        for i in live:
            if slot < i * skew:
                nxt.append(i)
                continue
            try:
                next(gens[i])
                nxt.append(i)
            except StopIteration:
                pass
        live = nxt
        slot += 1


def _loop_streams(n, rows, chunk_gen, depth, skew):
    def trip(i, carry):
        _lockstep([chunk_gen(r, depth * i + d) for d in range(depth) for r in range(rows)], skew)
        return carry

    if n >= depth:
        lax.fori_loop(0, n // depth, trip, 0)
    tail = n % depth
    if tail:
        _lockstep([chunk_gen(r, n - tail + d) for d in range(tail) for r in range(rows)], skew)


def _chunk_rows(c):
    start = c * CHUNK
    return pl.ds(start if isinstance(c, int) else pl.multiple_of(start, CHUNK), CHUNK)


def _const_spec(shape):
    nd = len(shape)
    return pl.BlockSpec(shape, lambda *_: (0,) * nd, pipeline_mode=pl.Buffered(1))


def _row_spec(rows, width, nb=1):
    return pl.BlockSpec((nb, rows, width), lambda b, j: (b, j, 0))


STREAM_ROWS = 4
STREAM_DEPTH = 5
RW_SKEW = 1
GLA_SKEW = 0


def _stream_rows(b):
    return STREAM_ROWS if b % STREAM_ROWS == 0 else 1


def _params(sem):
    return pltpu.CompilerParams(dimension_semantics=sem, vmem_limit_bytes=VMEM_LIMIT)


def _pick_tile(lp, cap, step=CHUNK):
    best = step
    t = step
    while t <= min(lp, cap):
        if lp % t == 0:
            best = t
        t += step
    return best


def _proj_kernel(h_ref, g_ref, w_ref, *out_refs, widths):
    xn = _rmsnorm_rows(h_ref[0], g_ref[...]).astype(BF16)
    off = 0
    for o_ref, w in zip(out_refs, widths):
        o_ref[0] = jnp.dot(xn, w_ref[:, off:off + w], preferred_element_type=F32)
        off += w


def _proj(h, g, w, widths, tm):
    b, lp, d = h.shape
    nt = lp // tm
    return pl.pallas_call(
        functools.partial(_proj_kernel, widths=widths),
        grid=(b, nt),
        in_specs=[_row_spec(tm, d), _const_spec((1, d)), _const_spec(w.shape)],
        out_specs=[_row_spec(tm, wd) for wd in widths],
        out_shape=[jax.ShapeDtypeStruct((b, lp, wd), F32) for wd in widths],
        compiler_params=_params(("parallel", "parallel")),
        name="proj",
    )(h, g, w)


def _cmul(pr, pi, zr, zi):
    return pr * zr - pi * zi, pr * zi + pi * zr


def _s5_kernel(u_ref, kd_ref, bb_ref, cc_ref, pe_ref, pc_ref, a_ref, y_ref,
               io_scr, xs_scr, ys_scr, e_scr, sp_scr, *, nb, nbp):
    ns = S5_GROUPS * S5_STATE
    pad = nbp - nb
    halves = BRANCH_W // LANES
    for hf in range(halves):
        io_scr[hf] = u_ref[0, :, hf * LANES:(hf + 1) * LANES]
    for s in range(S5_BLOCK):
        for hf in range(halves):
            xs_scr[s, 0:nb, hf * LANES:(hf + 1) * LANES] = (
                io_scr[hf, pl.ds(s, nb, stride=S5_BLOCK), :].astype(BF16))
        if pad:
            xs_scr[s, nb:nbp, :] = jnp.zeros((pad, BRANCH_W), BF16)
    if pad:
        sp_scr[nb:nbp, :] = jnp.zeros((pad, 2 * ns), F32)

    e_re = None
    for s in range(S5_BLOCK):
        z = jnp.dot(xs_scr[s], bb_ref[...], preferred_element_type=F32)
        zr, zi = _cmul(pe_ref[s:s + 1, 0:ns], pe_ref[s:s + 1, ns:], z[:, 0:ns], z[:, ns:])
        e_re, e_im = (zr, zi) if e_re is None else (e_re + zr, e_im + zi)
    e_scr[:, 0:ns] = e_re
    e_scr[:, ns:] = e_im

    ar = a_ref[0:1, 0:ns]
    ai = a_ref[0:1, ns:]

    def step(k, carry):
        sr, si = carry
        sp_scr[pl.ds(k, 1), 0:ns] = sr
        sp_scr[pl.ds(k, 1), ns:] = si
        nr, ni = _cmul(ar, ai, sr, si)
        return nr + e_scr[pl.ds(k, 1), 0:ns], ni + e_scr[pl.ds(k, 1), ns:]

    zero = jnp.zeros((1, ns), F32)
    lax.fori_loop(0, nb, step, (zero, zero))

    for d in range(S5_BLOCK):
        m = S5_BLOCK - d
        prod = jnp.dot(xs_scr[0:m].reshape(m * nbp, BRANCH_W), kd_ref[d], preferred_element_type=F32)
        prod = prod.reshape(m, nbp, BRANCH_W)
        if d == 0:
            ys_scr[...] = prod
        else:
            ys_scr[d:S5_BLOCK] = ys_scr[d:S5_BLOCK] + prod

    spr = sp_scr[:, 0:ns]
    spi = sp_scr[:, ns:]
    for t in range(S5_BLOCK):
        gr, gi = _cmul(pc_ref[t:t + 1, 0:ns], pc_ref[t:t + 1, ns:], spr, spi)
        g = jnp.concatenate([gr, gi], axis=1).astype(BF16)
        yt = ys_scr[t] + jnp.dot(g, cc_ref[...], preferred_element_type=F32)
        for hf in range(halves):
            io_scr[hf, pl.ds(t, nb, stride=S5_BLOCK), :] = yt[0:nb, hf * LANES:(hf + 1) * LANES]
    for hf in range(halves):
        y_ref[0, :, hf * LANES:(hf + 1) * LANES] = io_scr[hf]


def _s5_mats(a_re, a_im, log_dt, b_re, b_im, c_re, c_im):
    lr, li = a_re.astype(F32), a_im.astype(F32)
    dt = jnp.exp(log_dt.astype(F32))[:, None]
    mag = jnp.exp(lr * dt)
    abr, abi = mag * jnp.cos(li * dt), mag * jnp.sin(li * dt)
    den = lr * lr + li * li
    cr = ((abr - 1.0) * lr + abi * li) / den
    ci = (abi * lr - (abr - 1.0) * li) / den
    d = jnp.arange(S5_BLOCK + 1, dtype=F32)[:, None, None]
    pmag = jnp.exp(lr * dt * d)
    p_re, p_im = pmag * jnp.cos(li * dt * d), pmag * jnp.sin(li * dt * d)
    bb_re = cr[..., None] * b_re - ci[..., None] * b_im
    bb_im = cr[..., None] * b_im + ci[..., None] * b_re
    cp_re = c_re[None] * p_re[:, :, None, :] - c_im[None] * p_im[:, :, None, :]
    cp_im = c_re[None] * p_im[:, :, None, :] + c_im[None] * p_re[:, :, None, :]
    hp = lax.Precision.HIGHEST
    kd = (jnp.einsum("dgon,gni->dgoi", cp_re, bb_re, precision=hp)
          - jnp.einsum("dgon,gni->dgoi", cp_im, bb_im, precision=hp))
    eye_g = jnp.eye(S5_GROUPS, dtype=F32)
    kmat = jnp.einsum("dgoi,gq->dgiqo", kd[:S5_BLOCK], eye_g).reshape(S5_BLOCK, BRANCH_W, BRANCH_W)
    ns = S5_GROUPS * S5_STATE
    bmat = jnp.concatenate([jnp.einsum("gni,gq->giqn", bb_re, eye_g).reshape(BRANCH_W, ns),
                            jnp.einsum("gni,gq->giqn", bb_im, eye_g).reshape(BRANCH_W, ns)], axis=1)
    cmat = jnp.concatenate([jnp.einsum("gon,gq->gnqo", c_re.astype(F32), eye_g).reshape(ns, BRANCH_W),
                            jnp.einsum("gon,gq->gnqo", -c_im.astype(F32), eye_g).reshape(ns, BRANCH_W)], axis=0)
    flat = lambda z: z.reshape(z.shape[0], ns)
    t = np.arange(S5_BLOCK)
    pe = jnp.concatenate([flat(p_re[S5_BLOCK - 1 - t]), flat(p_im[S5_BLOCK - 1 - t])], axis=1)
    pc = jnp.concatenate([flat(p_re[1:]), flat(p_im[1:])], axis=1)
    a16 = jnp.concatenate([flat(p_re[S5_BLOCK:]), flat(p_im[S5_BLOCK:])], axis=1)
    a16 = jnp.pad(a16, ((0, SUBLANES - 1), (0, 0)))
    return kmat.astype(BF16), bmat.astype(BF16), cmat.astype(BF16), pe, pc, a16


def _s5(p5, mats):
    kmat, bmat, cmat, pe, pc, a16 = mats
    b, lp, _ = p5.shape
    nb = lp // S5_BLOCK
    nbp = -(-nb // 16) * 16
    ns2 = 2 * S5_GROUPS * S5_STATE
    return pl.pallas_call(
        functools.partial(_s5_kernel, nb=nb, nbp=nbp),
        grid=(b,),
        in_specs=[pl.BlockSpec((1, lp, BRANCH_W), lambda i: (i, 0, 0)), _const_spec(kmat.shape),
                  _const_spec(bmat.shape), _const_spec(cmat.shape), _const_spec(pe.shape),
                  _const_spec(pc.shape), _const_spec(a16.shape)],
        out_specs=pl.BlockSpec((1, lp, BRANCH_W), lambda i: (i, 0, 0)),
        out_shape=jax.ShapeDtypeStruct((b, lp, BRANCH_W), F32),
        scratch_shapes=[pltpu.VMEM((BRANCH_W // LANES, lp, LANES), F32),
                        pltpu.VMEM((S5_BLOCK, nbp, BRANCH_W), BF16),
                        pltpu.VMEM((S5_BLOCK, nbp, BRANCH_W), F32),
                        pltpu.VMEM((nbp, ns2), F32),
                        pltpu.VMEM((nbp, ns2), F32)],
        compiler_params=_params(("parallel",)),
        name="s5",
    )(p5, kmat, bmat, cmat, pe, pc, a16)


def _gla_kernel(p_ref, wg_ref, bg_ref, nw_ref, y_ref, st_scr, *, cpt, nrows):
    @pl.when(pl.program_id(1) == 0)
    def _():
        st_scr[...] = jnp.zeros_like(st_scr)

    hq = GLA_HEADS * GLA_DK
    hv = GLA_HEADS * GLA_DV
    rows4 = GLA_HEADS * CHUNK
    tri = (_iota((CHUNK, CHUNK), 0) >= _iota((CHUNK, CHUNK), 1)).astype(BF16)
    bd_k = _div(_iota((rows4, hq), 0), CHUNK) == _div(_iota((rows4, hq), 1), GLA_DK)
    causal = _mod(_iota((CHUNK, rows4), 1), CHUNK) <= _iota((CHUNK, rows4), 0)
    bd_v = _div(_iota((rows4, hv), 0), CHUNK) == _div(_iota((rows4, hv), 1), GLA_DV)
    bd_st = _div(_iota((hv, hq), 0), GLA_DV) == _div(_iota((hv, hq), 1), GLA_DK)
    head_ones = (_div(_iota((hv, hv), 0), GLA_DV) == _div(_iota((hv, hv), 1), GLA_DV)).astype(BF16)
    wg = wg_ref[...]
    bg = bg_ref[...]
    nw = nw_ref[...]

    def chunk(bi, c):
        rows = _chunk_rows(c)
        p = p_ref[bi, rows, :]
        q = p[:, 0:hq]
        k = p[:, hq:2 * hq]
        v = p[:, 2 * hq:2 * hq + hv]
        og = p[:, 2 * hq + hv:2 * hq + 2 * hv]
        gd = p[:, 2 * hq + 2 * hv:]
        x = _mm3(gd, wg) + bg
        yield
        g = (jnp.minimum(x, 0.0) - jnp.log1p(jnp.exp(-jnp.abs(x)))) / GLA_TAU
        gc = _mm_exact_lhs(tri, g)
        yield
        gl = gc[CHUNK - 1:CHUNK, :]
        qg = q * (GLA_DK ** -0.5) * jnp.exp(gc)
        kg = k * jnp.exp(-gc)
        kl = k * jnp.exp(gl - gc)
        kg_e = _expand_heads(kg, bd_k)
        att = _mm_nt(qg, kg_e)
        kv = _mm_tn(v, kl)
        yield
        att = jnp.where(causal, att, 0.0)
        v_e = _expand_heads(v, bd_v)
        st = st_scr[bi]
        o = _mm(att, v_e) + _mm_nt(qg, st)
        st_scr[bi] = st * jnp.exp(gl) + jnp.where(bd_st, kv, 0.0)
        yield
        ms = _mm(o * o, head_ones) * (1.0 / GLA_DV)
        yield
        o = o * lax.rsqrt(ms + RMS_EPS) * nw
        y_ref[bi, rows, :] = o * _silu(og)

    _loop_streams(cpt, nrows, chunk, STREAM_DEPTH, GLA_SKEW)


def _gla(pg, w_gate_pad, b_gate, norm_w, tile):
    b, lp, _ = pg.shape
    cpt = tile // CHUNK
    nrows = _stream_rows(b)
    return pl.pallas_call(
        functools.partial(_gla_kernel, cpt=cpt, nrows=nrows),
        grid=(b // nrows, lp // tile),
        in_specs=[_row_spec(tile, GLA_W, nrows), _const_spec(w_gate_pad.shape),
                  _const_spec(b_gate.shape), _const_spec(norm_w.shape)],
        out_specs=_row_spec(tile, BRANCH_W, nrows),
        out_shape=jax.ShapeDtypeStruct((b, lp, BRANCH_W), F32),
        scratch_shapes=[pltpu.VMEM((nrows, GLA_HEADS * GLA_DV, GLA_HEADS * GLA_DK), F32)],
        compiler_params=_params(("parallel", "arbitrary")),
        name="gla",
    )(pg, w_gate_pad, b_gate, norm_w)


def _rw_kernel(*refs, cpt, nrows, has_vgate):
    if has_vgate:
        (p_ref, vf_ref, mu_ref, w0_ref, w2_ref, a0_ref, a2_ref, g2_ref, kk_ref, ka_ref, rk_ref,
         lnw_ref, lnb_ref, v0_ref, v2_ref, y_ref, carry_scr, h_scr) = refs
    else:
        (p_ref, mu_ref, w0_ref, w2_ref, a0_ref, a2_ref, g2_ref, kk_ref, ka_ref, rk_ref,
         lnw_ref, lnb_ref, y_ref, vf_out_ref, carry_scr, h_scr) = refs

    @pl.when(pl.program_id(1) == 0)
    def _():
        carry_scr[...] = jnp.zeros_like(carry_scr)
        h_scr[...] = jnp.zeros_like(h_scr)

    w = BRANCH_W
    rows4 = RW_HEADS * CHUNK
    r_i = _iota((rows4, w), 0)
    c_i = _iota((rows4, w), 1)
    bd = _div(r_i, CHUNK) == _div(c_i, RW_HEAD)
    eye = (r_i == c_i).astype(F32)
    t_row = _iota((CHUNK, w), 0)
    s_col = _mod(_iota((CHUNK, w), 1), CHUNK)
    strict = s_col < t_row
    incl = s_col <= t_row
    eye_c = (s_col == t_row).astype(F32)
    tri = (_iota((CHUNK, CHUNK), 0) >= _iota((CHUNK, CHUNK), 1)).astype(BF16)
    head_ones = bd.astype(BF16)
    row0 = _iota((CHUNK, RW_W), 0) == 0
    zeros_cw = jnp.zeros((CHUNK, w), F32)
    mu = mu_ref[...]
    inv_n = 1.0 / RW_HEAD

    def expand(x):
        return _expand_heads(x, bd)

    def headsum(x):
        return _mm(x, head_ones)

    def chunk(bi, c):
        rows = _chunk_rows(c)
        p = p_ref[bi, rows, :]
        pf = p[:, 0:RW_W]
        prev = jnp.where(row0, carry_scr[bi, 0:1, :], pltpu.roll(pf, 1, axis=0))
        carry_scr[bi, 0:1, :] = pf[CHUNK - 1:CHUNK, :]
        xs = pf + mu * (prev - pf)
        r = xs[:, 0:w]
        k = xs[:, w:2 * w]
        v = xs[:, 2 * w:3 * w]
        lr = xs[:, 3 * w:]
        wlow = _mm3(jnp.tanh(lr), w2_ref[...])
        alow = _mm(lr, a2_ref[...])
        g = _mm(_sigmoid(lr), g2_ref[...])
        if has_vgate:
            vlow = _mm(p[:, RW_W:], v2_ref[...])
        kk = k * kk_ref[...]
        kss = headsum(kk * kk)
        yield
        w_log = -_softplus(-(w0_ref[...] + wlow)) - 0.5
        lw = -jnp.exp(w_log)
        cum = _mm_exact_lhs(tri, lw)
        ag = _sigmoid(a0_ref[...] + alow)
        if has_vgate:
            v = v + (vf_ref[bi, rows, :] - v) * _sigmoid(v0_ref[...] + vlow)
        else:
            vf_out_ref[bi, rows, :] = v
        kk = kk * lax.rsqrt(jnp.maximum(kss, 1e-24))
        k2 = k * (1.0 + (ag - 1.0) * ka_ref[...])
        bvec = kk * ag
        rk_sum = headsum(r * k2 * rk_ref[...])
        v_e = expand(v)
        yield
        tot = cum[CHUNK - 1:CHUNK, :]
        e_neg = jnp.exp(-cum)
        e_tail = jnp.exp(tot - cum)
        a_n = -kk * jnp.exp(cum - lw)
        r_n = r * jnp.exp(cum)
        lhs = jnp.concatenate([a_n, r_n], axis=0)
        rhs = jnp.concatenate([expand(bvec * e_neg), expand(k2 * e_neg)], axis=0)
        aa = _mm_nt(lhs, rhs)
        yield
        a_ab = jnp.where(strict, aa[0:CHUNK, 0:w], 0.0)
        a_ak = jnp.where(strict, aa[0:CHUNK, w:], 0.0)
        a_rb = jnp.where(incl, aa[CHUNK:, 0:w], 0.0)
        a_rk = jnp.where(incl, aa[CHUNK:, w:], 0.0)
        tinv = eye_c + a_ab
        pw = a_ab
        pw_e = expand(pw)
        akv = _mm(a_ak, v_e)
        rkv = _mm(a_rk, v_e)
        for _ in range(5):
            pw = _mm(pw, pw_e)
            yield
            pw_e = expand(pw)
            tinv = tinv + _mm(tinv, pw_e)
        yield
        x_n = _mm(tinv, jnp.concatenate([expand(a_n), expand(akv)], axis=1))
        yield
        yq = _mm(a_rb, jnp.concatenate([expand(x_n[:, 0:w]), expand(x_n[:, w:])], axis=1))
        mn = _mm_tn(jnp.concatenate([bvec * e_tail, k2 * e_tail], axis=0),
                    jnp.concatenate([x_n, jnp.concatenate([zeros_cw, v], axis=1)], axis=0))
        yield
        q_n = r_n + yq[:, 0:w]
        ol_n = yq[:, w:] + rkv
        m_mat = jnp.where(bd, mn[:, 0:w], 0.0) + eye * jnp.exp(tot)
        n_mat = jnp.where(bd, mn[:, w:], 0.0)
        h = h_scr[bi]
        o = ol_n + _mm(q_n, h)
        h_scr[bi] = _mm(m_mat, h) + n_mat
        yield
        mean = headsum(o) * inv_n
        yield
        dlt = o - mean
        var = headsum(dlt * dlt) * inv_n
        yield
        o = dlt * lax.rsqrt(var + RW_LN_EPS) * lnw_ref[...] + lnb_ref[...]
        y_ref[bi, rows, :] = (o + rk_sum * v) * g

    _loop_streams(cpt, nrows, chunk, STREAM_DEPTH, RW_SKEW)


def _rwkv(pr, v_first, prm, tile):
    b, lp, pw = pr.shape
    has_vgate = v_first is not None
    cpt = tile // CHUNK
    vec = _const_spec((1, BRANCH_W))
    mat = _const_spec((LANES, BRANCH_W))
    nrows = _stream_rows(b)
    ins = [pr]
    specs = [_row_spec(tile, pw, nrows)]
    if has_vgate:
        ins.append(v_first)
        specs.append(_row_spec(tile, BRANCH_W, nrows))
    ins += [prm["mu"], prm["w0"], prm["w2"], prm["a0"], prm["a2"], prm["g2"], prm["k_k"], prm["k_a"],
            prm["r_k"], prm["ln_w"], prm["ln_b"]]
    specs += [_const_spec((1, RW_W)), vec, mat, vec, mat, mat, vec, vec, vec, vec, vec]
    out_shape = [jax.ShapeDtypeStruct((b, lp, BRANCH_W), F32)]
    out_specs = [_row_spec(tile, BRANCH_W, nrows)]
    if has_vgate:
        ins += [prm["v0"], prm["v2"]]
        specs += [vec, mat]
    else:
        out_shape.append(jax.ShapeDtypeStruct((b, lp, BRANCH_W), F32))
        out_specs.append(_row_spec(tile, BRANCH_W, nrows))
    outs = pl.pallas_call(
        functools.partial(_rw_kernel, cpt=cpt, nrows=nrows, has_vgate=has_vgate),
        grid=(b // nrows, lp // tile),
        in_specs=specs,
        out_specs=out_specs,
        out_shape=out_shape,
        scratch_shapes=[pltpu.VMEM((nrows, SUBLANES, RW_W), F32),
                        pltpu.VMEM((nrows, RW_HEADS * RW_HEAD, BRANCH_W), F32)],
        compiler_params=_params(("parallel", "arbitrary")),
        name="rwkv7",
    )(*ins)
    if has_vgate:
        return outs[0], v_first
    return outs[0], outs[1]


def _lru_kernel(p_ref, cw_ref, cb_ref, wg_ref, bg_ref, lam_ref, y_ref, x_scr, h_scr, a_scr, b_scr, *, tl):
    j = pl.program_id(1)

    @pl.when(j == 0)
    def _():
        x_scr[...] = jnp.zeros_like(x_scr)
        h_scr[...] = jnp.zeros_like(h_scr)

    w = BRANCH_W
    p = p_ref[0]
    xr = p[:, 0:w]
    gate = p[:, w:]
    carry8 = x_scr[...]
    xc = cb_ref[...] + cw_ref[LRU_CONV_W - 1:LRU_CONV_W, :] * xr
    for s in range(1, LRU_CONV_W):
        xc = xc + cw_ref[LRU_CONV_W - 1 - s:LRU_CONV_W - s, :] * _shift_rows(xr, s, carry8)
    x_scr[...] = xr[tl - SUBLANES:tl, :]
    gates = _mm(xc, wg_ref[...]) + bg_ref[...]
    rg = _sigmoid(gates[:, 0:w])
    ig = _sigmoid(gates[:, w:])
    log_a = -LRU_C * rg * _softplus(-lam_ref[...])
    a = jnp.exp(log_a)
    mult = jnp.sqrt(1.0 - jnp.exp(2.0 * log_a))
    ridx = j * tl + _iota((tl, w), 0)
    mult = jnp.where(ridx == PAD, 1.0, mult)
    bterm = jnp.where(ridx >= PAD, mult * ig * xc, 0.0)

    ng = tl // SUBLANES
    g_i = _iota((ng, LANES), 0)
    for hf in range(w // LANES):
        lanes = slice(hf * LANES, (hf + 1) * LANES)
        a_scr[hf] = a[:, lanes]
        b_scr[hf] = bterm[:, lanes]
        prods, sums = [], []
        for r in range(SUBLANES):
            a_r = a_scr[hf, pl.ds(r, ng, stride=SUBLANES), :]
            b_r = b_scr[hf, pl.ds(r, ng, stride=SUBLANES), :]
            prods.append(a_r if r == 0 else a_r * prods[-1])
            sums.append(b_r if r == 0 else a_r * sums[-1] + b_r)
        pt, ct = prods[-1], sums[-1]
        d = 1
        while d < ng:
            keep = g_i >= d
            ct = ct + pt * jnp.where(keep, pltpu.roll(ct, d, axis=0), 0.0)
            pt = pt * jnp.where(keep, pltpu.roll(pt, d, axis=0), 1.0)
            d *= 2
        h_end = ct + pt * h_scr[0:1, lanes]
        h_in = jnp.where(g_i == 0, h_scr[0:1, lanes], pltpu.roll(h_end, 1, axis=0))
        h_scr[0:1, lanes] = h_end[ng - 1:ng, :]
        for r in range(SUBLANES):
            b_scr[hf, pl.ds(r, ng, stride=SUBLANES), :] = sums[r] + prods[r] * h_in
        y_ref[0, :, lanes] = b_scr[hf] * _gelu(gate[:, lanes])


def _lru(pl_, cw, cb, wg, bg, lam, tile):
    b, lp, _ = pl_.shape
    vec = _const_spec((1, BRANCH_W))
    return pl.pallas_call(
        functools.partial(_lru_kernel, tl=tile),
        grid=(b, lp // tile),
        in_specs=[_row_spec(tile, LRU_W), _const_spec(cw.shape), vec, _const_spec(wg.shape),
                  _const_spec(bg.shape), vec],
        out_specs=_row_spec(tile, BRANCH_W),
        out_shape=jax.ShapeDtypeStruct((b, lp, BRANCH_W), F32),
        scratch_shapes=[pltpu.VMEM((SUBLANES, BRANCH_W), F32), pltpu.VMEM((SUBLANES, BRANCH_W), F32),
                        pltpu.VMEM((BRANCH_W // LANES, tile, LANES), F32),
                        pltpu.VMEM((BRANCH_W // LANES, tile, LANES), F32)],
        compiler_params=_params(("parallel", "arbitrary")),
        name="rglru",
    )(pl_, cw, cb, wg, bg, lam)


MERGE_COLS = 256


def _merge_kernel(h_ref, g_ref, u5_ref, y5_ref, yg_ref, yr_ref, yl_ref, d5_ref, wglu_ref, bglu_ref,
                  wgate_ref, bgate_ref, wbr_ref, wout_ref, o_ref, m_scr, *, tm):
    x = h_ref[0]
    xn = _rmsnorm_rows(x, g_ref[...]).astype(BF16)
    ys = _gelu(y5_ref[0] + d5_ref[...] * u5_ref[0])
    ys = ys * _sigmoid(_mm(ys, wglu_ref[...]) + bglu_ref[...])
    branches = tuple(yb.astype(BF16) for yb in (ys, yg_ref[0], yr_ref[0], yl_ref[0]))
    for c in range(D_MODEL // MERGE_COLS):
        cols = slice(c * MERGE_COLS, (c + 1) * MERGE_COLS)
        merged = None
        for i, yb in enumerate(branches):
            gate = _sigmoid(jnp.dot(xn, wgate_ref[i, :, cols], preferred_element_type=F32)
                            + bgate_ref[i, :, cols])
            term = gate * jnp.dot(yb, wbr_ref[i, :, cols], preferred_element_type=F32)
            merged = term if merged is None else merged + term
        m_scr[:, cols] = merged.astype(BF16)
    upd = jnp.dot(m_scr[...], wout_ref[...], preferred_element_type=F32)
    ridx = pl.program_id(1) * tm + _iota((tm, 1), 0)
    o_ref[0] = x + jnp.where(ridx >= PAD, upd, 0.0)


def _merge(h, g, u5, y5, yg, yr, yl, d5, wglu, bglu, wgate, bgate, wbr, wout, tm):
    b, lp, d = h.shape
    ybs = _row_spec(tm, BRANCH_W)
    return pl.pallas_call(
        functools.partial(_merge_kernel, tm=tm),
        grid=(b, lp // tm),
        in_specs=[_row_spec(tm, d), _const_spec((1, d)), ybs, ybs, ybs, ybs, ybs,
                  _const_spec((1, BRANCH_W)), _const_spec(wglu.shape), _const_spec((1, BRANCH_W)),
                  _const_spec(wgate.shape), _const_spec(bgate.shape), _const_spec(wbr.shape),
                  _const_spec(wout.shape)],
        out_specs=_row_spec(tm, d),
        out_shape=jax.ShapeDtypeStruct((b, lp, d), F32),
        scratch_shapes=[pltpu.VMEM((tm, d), BF16)],
        compiler_params=_params(("parallel", "parallel")),
        name="merge",
    )(h, g, u5, y5, yg, yr, yl, d5, wglu, bglu, wgate, bgate, wbr, wout)


FFN_COLS = 256


def _ffn_kernel(*refs, tm, final):
    if final:
        h_ref, g_ref, wup_ref, cw_ref, cb_ref, wdn_ref, gf_ref, o_ref, g_scr, act_scr = refs
    else:
        h_ref, g_ref, wup_ref, cw_ref, cb_ref, wdn_ref, o_ref, g_scr, act_scr = refs
    j = pl.program_id(1)

    @pl.when(j == 0)
    def _():
        g_scr[...] = jnp.zeros_like(g_scr)

    x = h_ref[0]
    xn = _rmsnorm_rows(x, g_ref[...]).astype(BF16)
    for c in range(D_FF // FFN_COLS):
        lo = c * FFN_COLS
        gate = jnp.dot(xn, wup_ref[:, lo:lo + FFN_COLS], preferred_element_type=F32)
        up = jnp.dot(xn, wup_ref[:, D_FF + lo:D_FF + lo + FFN_COLS], preferred_element_type=F32)
        carry8 = g_scr[:, lo:lo + FFN_COLS]
        g_scr[:, lo:lo + FFN_COLS] = gate[tm - SUBLANES:tm, :]
        gc = cb_ref[:, lo:lo + FFN_COLS] + cw_ref[FFN_CONV_W - 1:FFN_CONV_W, lo:lo + FFN_COLS] * gate
        for s in range(1, FFN_CONV_W):
            gc = gc + (cw_ref[FFN_CONV_W - 1 - s:FFN_CONV_W - s, lo:lo + FFN_COLS]
                       * _shift_rows(gate, s, carry8))
        act_scr[:, lo:lo + FFN_COLS] = (_silu(gc) * up).astype(BF16)
    acc = jnp.dot(act_scr[...], wdn_ref[...], preferred_element_type=F32)
    ridx = j * tm + _iota((tm, 1), 0)
    out = x + jnp.where(ridx >= PAD, acc, 0.0)
    if final:
        out = _rmsnorm_rows(out, gf_ref[...])
    o_ref[0] = out


def _ffn(h, g, wup, cw, cb, wdn, gf, tm):
    b, lp, d = h.shape
    final = gf is not None
    ins = [h, g, wup, cw, cb, wdn]
    specs = [_row_spec(tm, d), _const_spec((1, d)), _const_spec(wup.shape), _const_spec(cw.shape),
             _const_spec(cb.shape), _const_spec(wdn.shape)]
    if final:
        ins.append(gf)
        specs.append(_const_spec((1, d)))
    return pl.pallas_call(
        functools.partial(_ffn_kernel, tm=tm, final=final),
        grid=(b, lp // tm),
        in_specs=specs,
        out_specs=_row_spec(tm, d),
        out_shape=jax.ShapeDtypeStruct((b, lp, d), F32),
        scratch_shapes=[pltpu.VMEM((SUBLANES, D_FF), F32), pltpu.VMEM((tm, D_FF), BF16)],
        compiler_params=_params(("parallel", "arbitrary")),
        name="ffn",
    )(*ins)


def _pad_rows(m, rows, at=0):
    return jnp.zeros((rows, m.shape[1]), m.dtype).at[at:at + m.shape[0]].set(m)


def _block_diag(w):
    hn, n, _ = w.shape
    out = jnp.zeros((hn * n, hn * n), w.dtype)
    for i in range(hn):
        out = out.at[i * n:(i + 1) * n, i * n:(i + 1) * n].set(w[i])
    return out


def kernel(x, meta, w_in, rw_mu, s5_a_re, s5_a_im, s5_log_dt, s5_b_re, s5_b_im, s5_c_re, s5_c_im, s5_d, s5_w_glu, s5_b_glu, gla_w_gate, gla_b_gate, gla_norm, rw_w0, rw_w2, rw_a0, rw_a2, rw_g2, rw_k_k, rw_k_a, rw_r_k, rw_ln_w, rw_ln_b, rw_v0, rw_v1, rw_v2, lru_conv_w, lru_conv_b, lru_w_a, lru_b_a, lru_w_i, lru_b_i, lru_lam, w_branch, w_merge_gate, b_merge_gate, w_out, norm_mix, norm_ffn, ffn_w_up, ffn_conv_w, ffn_conv_b, ffn_w_down, norm_final):
    bsz, seq, d = x.shape
    assert d == D_MODEL and seq % CHUNK == 0
    lp = FRONT + seq
    tm = _pick_tile(lp, 1040, SUBLANES)
    tchunk = _pick_tile(lp, 832)
    tstream = _pick_tile(lp, STREAM_DEPTH * CHUNK)
    row = lambda v: v.astype(F32).reshape(1, -1)

    h = jnp.concatenate([jnp.zeros((bsz, PAD, d), x.dtype),
                         jnp.broadcast_to(meta.astype(x.dtype)[None], (bsz, N_META, d)), x], axis=1)
    v_first = None
    for l in range(DEPTH):
        wl = w_in[l]
        o_gla = BRANCH_W
        o_rw = o_gla + 784
        o_lru = o_rw + 896
        hq = GLA_HEADS * GLA_DK
        gla_cols = jnp.concatenate([
            wl[:, o_gla:o_gla + 2 * hq + BRANCH_W],
            wl[:, o_gla + 2 * hq + BRANCH_W + GLA_GATE_RANK:o_rw],
            wl[:, o_gla + 2 * hq + BRANCH_W:o_gla + 2 * hq + BRANCH_W + GLA_GATE_RANK],
            jnp.zeros((d, LANES - GLA_GATE_RANK), wl.dtype)], axis=1)
        rw_cols = wl[:, o_rw:o_lru]
        if l > 0:
            rw_cols = jnp.concatenate([rw_cols, rw_v1[l - 1],
                                       jnp.zeros((d, LANES - RW_V_RANK), wl.dtype)], axis=1)
        wcat = jnp.concatenate([wl[:, 0:BRANCH_W], gla_cols, rw_cols, wl[:, o_lru:]], axis=1).astype(BF16)
        widths = (BRANCH_W, GLA_W, RW_WV if l > 0 else RW_W, LRU_W)
        p5, pg, pr, pu = _proj(h, row(norm_mix[l]), wcat, widths, tm)

        y5 = _s5(p5, _s5_mats(s5_a_re[l], s5_a_im[l], s5_log_dt[l], s5_b_re[l], s5_b_im[l],
                              s5_c_re[l], s5_c_im[l]))
        yg = _gla(pg, _pad_rows(gla_w_gate[l].astype(F32), LANES), row(gla_b_gate[l]), row(gla_norm[l]), tstream)
        prm = dict(mu=row(rw_mu[l]), w0=row(rw_w0[l]), w2=_pad_rows(rw_w2[l].astype(F32), LANES, 0),
                   a0=row(rw_a0[l]), a2=_pad_rows(rw_a2[l].astype(F32), LANES, RW_W_RANK),
                   g2=_pad_rows(rw_g2[l].astype(F32), LANES, RW_W_RANK + RW_A_RANK),
                   k_k=row(rw_k_k[l]), k_a=row(rw_k_a[l]), r_k=row(rw_r_k[l]),
                   ln_w=row(rw_ln_w[l]), ln_b=row(rw_ln_b[l]))
        if l > 0:
            prm["v0"] = row(rw_v0[l - 1])
            prm["v2"] = _pad_rows(rw_v2[l - 1].astype(F32), LANES)
        yr, v_first = _rwkv(pr, v_first, prm, tstream)
        wgi = jnp.concatenate([_block_diag(lru_w_a[l]), _block_diag(lru_w_i[l])], axis=1).astype(BF16)
        bgi = jnp.concatenate([lru_b_a[l], lru_b_i[l]]).astype(F32).reshape(1, -1)
        yl = _lru(pu, lru_conv_w[l].astype(F32), row(lru_conv_b[l]), wgi, bgi, row(lru_lam[l]), tchunk)

        h = _merge(h, row(norm_mix[l]), p5, y5, yg, yr, yl, row(s5_d[l]), s5_w_glu[l].astype(BF16),
                   row(s5_b_glu[l]), w_merge_gate[l].astype(BF16),
                   b_merge_gate[l].astype(F32).reshape(4, 1, d), w_branch[l].astype(BF16),
                   w_out[l].astype(BF16), tm)
        h = _ffn(h, row(norm_ffn[l]), ffn_w_up[l].astype(BF16), ffn_conv_w[l].astype(F32),
                 row(ffn_conv_b[l]), ffn_w_down[l].astype(BF16),
                 row(norm_final) if l == DEPTH - 1 else None, tm)
    return h[:, FRONT:]
```
